```python
import jax, jax.numpy as jnp
from jax import lax
import numpy as np

D_MODEL = 2048
BATCH = 8
SEQ = 2048
DEPTH = 1

NSA_HEADS = 16
NSA_HEAD_DIM = 64
NSA_KV_GROUPS = 4
NSA_WIDTH = NSA_HEADS * NSA_HEAD_DIM
NSA_KV_WIDTH = NSA_KV_GROUPS * NSA_HEAD_DIM
CMP_LEN = 32
CMP_STRIDE = 16
SEL_LEN = 64
N_SEL = 8
WINDOW = 512
Q_BLOCK = 64
LRU_WIDTH = D_MODEL - NSA_WIDTH
LRU_BLOCKS = 16
LRU_BLOCK_W = LRU_WIDTH // LRU_BLOCKS
CONV_W = 4
RG_C = 8.0
N_EXPERTS = 32
TOP_K = 4
D_FF_EXPERT = D_MODEL
SWIGLU_LIMIT = 7.0
SWIGLU_ALPHA = 1.702
MOE_BLOCK = 256
NORM_EPS = 1e-6
NEG_INF = -1e30
SEL_FORCE = 1e30
IN_WIDTH = NSA_WIDTH + 6 * NSA_KV_WIDTH + 3 * NSA_HEADS + 2 * LRU_WIDTH

kernel_name = 'hymba_nsa_rglru_moe_adaln_layer'


def rms_norm(x, g):
    xf = x.astype(jnp.float32)
    y = xf * lax.rsqrt(jnp.mean(xf * xf, axis=-1, keepdims=True) + NORM_EPS)
    return (y * g.astype(jnp.float32)).astype(x.dtype)


def masked_softmax(s, mask):
    return jax.nn.softmax(jnp.where(mask, s, NEG_INF), axis=-1) * mask


def compress_blocks(k, pe, w):
    B, S, G, HD = k.shape
    n_sub = CMP_LEN // CMP_STRIDE
    sub = k.reshape(B, S // CMP_STRIDE, CMP_STRIDE, G, HD)
    n_cmp = S // CMP_STRIDE - n_sub + 1
    blocks = jnp.concatenate([sub[:, j:j + n_cmp] for j in range(n_sub)], axis=2)
    blocks = blocks + pe[None, None, :, None, :]
    flat = blocks.transpose(0, 1, 3, 2, 4).reshape(B, n_cmp, G, CMP_LEN * HD)
    return flat @ w


def nsa_mixer(q, k_c, v_c, k_s, v_s, k_w, v_w, gate_logits, pe_k, pe_v, w_ck, w_cv, q_gain, k_gain):
    B, S, H, HD = q.shape
    G = NSA_KV_GROUPS
    HG = H // G
    dt = q.dtype
    scale = NSA_HEAD_DIM ** -0.5
    qg = rms_norm(q, q_gain).reshape(B, S, G, HG, HD)
    t_all = jnp.arange(S)

    kc = rms_norm(compress_blocks(k_c, pe_k, w_ck), k_gain[0])
    vc = compress_blocks(v_c, pe_v, w_cv)
    n_cmp = kc.shape[1]
    cmp_start = jnp.arange(n_cmp) * CMP_STRIDE
    m_c = (cmp_start + CMP_LEN - 1)[None, :] <= t_all[:, None]
    s_c = jnp.einsum('bsghd,bcgd->bghsc', qg, kc).astype(jnp.float32) * scale
    p_c = masked_softmax(s_c, m_c)
    o_c = jnp.einsum('bghsc,bcgd->bsghd', p_c.astype(dt), vc)

    n_blk = S // SEL_LEN
    n_sel = min(N_SEL, n_blk)
    sel_start = jnp.arange(n_blk) * SEL_LEN
    overlap = ((cmp_start[:, None] < sel_start[None, :] + SEL_LEN)
               & (cmp_start[:, None] + CMP_LEN > sel_start[None, :])).astype(jnp.float32)
    imp = jnp.einsum('bghsc,cj->bgsj', p_c, overlap)
    q_blk = t_all // SEL_LEN
    j = jnp.arange(n_blk)
    valid = j[None, :] <= q_blk[:, None]
    forced = (j[None, :] == 0) | (j[None, :] == q_blk[:, None]) | (j[None, :] == q_blk[:, None] - 1)
    imp = jnp.where(forced, SEL_FORCE, jnp.where(valid, imp, -SEL_FORCE))
    _, sel_idx = lax.top_k(imp, n_sel)

    k_blocks = rms_norm(k_s, k_gain[1]).reshape(B, n_blk, SEL_LEN, G, HD).transpose(0, 3, 1, 2, 4)
    v_blocks = v_s.reshape(B, n_blk, SEL_LEN, G, HD).transpose(0, 3, 1, 2, 4)
    pad = ((0, 0), (WINDOW, 0), (0, 0), (0, 0))
    k_wp = jnp.pad(rms_norm(k_w, k_gain[2]), pad)
    v_wp = jnp.pad(v_w, pad)
    b_ix = jnp.arange(B)[:, None, None, None]
    g_ix = jnp.arange(G)[None, :, None, None]
    n_keys = n_sel * SEL_LEN

    def query_block(qs):
        qc = lax.dynamic_slice_in_dim(qg, qs, Q_BLOCK, axis=1)
        t = qs + jnp.arange(Q_BLOCK)
        ic = lax.dynamic_slice_in_dim(sel_idx, qs, Q_BLOCK, axis=2)
        kg = k_blocks[b_ix, g_ix, ic].reshape(B, G, Q_BLOCK, n_keys, HD)
        vg = v_blocks[b_ix, g_ix, ic].reshape(B, G, Q_BLOCK, n_keys, HD)
        pos = (ic[..., None] * SEL_LEN + jnp.arange(SEL_LEN)).reshape(B, G, Q_BLOCK, n_keys)
        m_s = (pos <= t[None, None, :, None])[:, :, None]
        s_s = jnp.einsum('bqghd,bgqkd->bghqk', qc, kg).astype(jnp.float32) * scale
        o_s = jnp.einsum('bghqk,bgqkd->bqghd', masked_softmax(s_s, m_s).astype(dt), vg)
        kwc = lax.dynamic_slice_in_dim(k_wp, qs, WINDOW + Q_BLOCK, axis=1)
        vwc = lax.dynamic_slice_in_dim(v_wp, qs, WINDOW + Q_BLOCK, axis=1)
        kpos = qs - WINDOW + jnp.arange(WINDOW + Q_BLOCK)
        m_w = ((kpos[None, :] <= t[:, None]) & (kpos[None, :] > t[:, None] - WINDOW)
               & (kpos[None, :] >= 0))
        s_w = jnp.einsum('bqghd,bkgd->bghqk', qc, kwc).astype(jnp.float32) * scale
        o_w = jnp.einsum('bghqk,bkgd->bqghd', masked_softmax(s_w, m_w).astype(dt), vwc)
        return o_s, o_w

    o_s, o_w = lax.map(query_block, jnp.arange(S // Q_BLOCK) * Q_BLOCK)
    o_s = jnp.moveaxis(o_s, 0, 1).reshape(B, S, G, HG, HD)
    o_w = jnp.moveaxis(o_w, 0, 1).reshape(B, S, G, HG, HD)
    g = jax.nn.sigmoid(gate_logits.astype(jnp.float32)).astype(dt).reshape(B, S, G, HG, 3)
    o = g[..., 0:1] * o_c + g[..., 1:2] * o_s + g[..., 2:3] * o_w
    return o.reshape(B, S, H * HD)


def _linear_combine(e1, e2):
    a1, b1 = e1
    a2, b2 = e2
    return a1 * a2, a2 * b1 + b2


def rglru_mixer(xr, xg, conv_w, conv_b, w_rg, b_rg, w_ig, b_ig, lam):
    B, S, W = xr.shape
    xp = jnp.pad(xr, ((0, 0), (CONV_W - 1, 0), (0, 0)))
    xc = conv_b + sum(xp[:, k:k + S] * conv_w[k] for k in range(CONV_W))
    xb = xc.reshape(B, S, LRU_BLOCKS, LRU_BLOCK_W)
    r = jax.nn.sigmoid(jnp.einsum('bsnc,ncd->bsnd', xb, w_rg) + b_rg).reshape(B, S, W)
    i = jax.nn.sigmoid(jnp.einsum('bsnc,ncd->bsnd', xb, w_ig) + b_ig).reshape(B, S, W)
    log_a = (-RG_C * jax.nn.softplus(-lam.astype(jnp.float32))) * r.astype(jnp.float32)
    a = jnp.exp(log_a)
    u = jnp.sqrt(-jnp.expm1(2.0 * log_a)) * (i * xc).astype(jnp.float32)
    _, h = lax.associative_scan(_linear_combine, (a, u), axis=1)
    return h.astype(xr.dtype) * jax.nn.gelu(xg)


def moe_ffn(h, w_router, b_router, w1, b1, w2, b2):
    B, S, D = h.shape
    N = B * S
    A = N * TOP_K
    hf = h.reshape(N, D)
    logits = (hf @ w_router + b_router).astype(jnp.float32)
    top_v, top_e = lax.top_k(logits, TOP_K)
    gates = jax.nn.softmax(top_v, axis=-1)
    e_flat = top_e.reshape(A).astype(jnp.int32)
    tok_flat = jnp.arange(A, dtype=jnp.int32) // TOP_K
    g_flat = gates.reshape(A)
    order = jnp.argsort(e_flat)
    e_s, tok_s, g_s = e_flat[order], tok_flat[order], g_flat[order]
    counts = jnp.bincount(e_flat, length=N_EXPERTS)
    starts = jnp.cumsum(counts) - counts
    pcounts = (counts + MOE_BLOCK - 1) // MOE_BLOCK * MOE_BLOCK
    pends = jnp.cumsum(pcounts)
    pstarts = pends - pcounts
    dest = pstarts[e_s] + jnp.arange(A, dtype=jnp.int32) - starts[e_s]
    n_blocks = (A + N_EXPERTS * (MOE_BLOCK - 1) + MOE_BLOCK - 1) // MOE_BLOCK
    P = n_blocks * MOE_BLOCK
    buf_tok = jnp.zeros((P,), jnp.int32).at[dest].set(tok_s)
    buf_g = jnp.zeros((P,), jnp.float32).at[dest].set(g_s)
    blk_e = jnp.minimum(jnp.searchsorted(pends, jnp.arange(n_blocks) * MOE_BLOCK, side='right'),
                        N_EXPERTS - 1)

    def expert_block(args):
        e, tok, g = args
        xb = hf[tok]
        u = xb @ w1[e] + b1[e]
        u_glu, u_lin = jnp.split(u, 2, axis=-1)
        u_glu = jnp.minimum(u_glu, SWIGLU_LIMIT)
        u_lin = jnp.clip(u_lin, -SWIGLU_LIMIT, SWIGLU_LIMIT)
        act = u_glu * jax.nn.sigmoid(SWIGLU_ALPHA * u_glu) * (u_lin + 1)
        y = act @ w2[e] + b2[e]
        return y * g[:, None].astype(y.dtype)

    yb = lax.map(expert_block, (blk_e, buf_tok.reshape(n_blocks, MOE_BLOCK),
                                buf_g.reshape(n_blocks, MOE_BLOCK)))
    out = jnp.zeros((N, D), yb.dtype).at[buf_tok].add(yb.reshape(P, D))
    return out.reshape(B, S, D).astype(h.dtype)


def hybrid_layer(x, mod, g_norm1, w_in, pe_cmp_k, pe_cmp_v, w_cmp_k, w_cmp_v, q_gain, k_gain,
                 conv_w, conv_b, w_rg, b_rg, w_ig, b_ig, lru_lambda, g_out_nsa, g_out_lru,
                 w_out, g_norm2, w_router, b_router, w_e1, b_e1, w_e2, b_e2):
    B, S, D = x.shape
    shift1, scale1, gate1, shift2, scale2, gate2 = [m[:, None, :] for m in jnp.split(mod, 6, axis=-1)]
    h = rms_norm(x, g_norm1) * (1 + scale1) + shift1
    z = h @ w_in
    sizes = [NSA_WIDTH] + [NSA_KV_WIDTH] * 6 + [3 * NSA_HEADS, LRU_WIDTH, LRU_WIDTH]
    q, k_c, v_c, k_s, v_s, k_w, v_w, gl, xr, xg = jnp.split(z, np.cumsum(sizes)[:-1].tolist(), axis=-1)
    kv = lambda t: t.reshape(B, S, NSA_KV_GROUPS, NSA_HEAD_DIM)
    o_nsa = nsa_mixer(q.reshape(B, S, NSA_HEADS, NSA_HEAD_DIM), kv(k_c), kv(v_c), kv(k_s), kv(v_s),
                      kv(k_w), kv(v_w), gl, pe_cmp_k, pe_cmp_v, w_cmp_k, w_cmp_v, q_gain, k_gain)
    o_lru = rglru_mixer(xr, xg, conv_w, conv_b, w_rg, b_rg, w_ig, b_ig, lru_lambda)
    mix = jnp.concatenate([rms_norm(o_nsa, g_out_nsa), rms_norm(o_lru, g_out_lru)], axis=-1) @ w_out
    x = x + gate1 * mix
    h2 = rms_norm(x, g_norm2) * (1 + scale2) + shift2
    return x + gate2 * moe_ffn(h2, w_router, b_router, w_e1, b_e1, w_e2, b_e2)


def setup_inputs(seed: int = 0) -> dict:
    key = jax.random.key(seed)
    ks = jax.random.split(key, 30)
    L = DEPTH
    nrm = lambda k, shape, s: jax.random.normal(k, shape, jnp.float32) * s
    u = jax.random.uniform(ks[18], (L, LRU_WIDTH), jnp.float32, minval=0.9, maxval=0.999)
    a0 = u ** (1.0 / RG_C)
    return {
        'x': nrm(ks[0], (BATCH, SEQ, D_MODEL), 1.0),
        'c': nrm(ks[1], (BATCH, D_MODEL), 1.0),
        'w_ada': nrm(ks[2], (L, D_MODEL, 6 * D_MODEL), 0.5 * D_MODEL ** -0.5),
        'b_ada': nrm(ks[3], (L, 6 * D_MODEL), 0.02),
        'g_norm1': 1.0 + nrm(ks[4], (L, D_MODEL), 0.02),
        'w_in': nrm(ks[5], (L, D_MODEL, IN_WIDTH), D_MODEL ** -0.5),
        'pe_cmp_k': nrm(ks[6], (L, CMP_LEN, NSA_HEAD_DIM), 0.1),
        'pe_cmp_v': nrm(ks[7], (L, CMP_LEN, NSA_HEAD_DIM), 0.1),
        'w_cmp_k': nrm(ks[8], (L, CMP_LEN * NSA_HEAD_DIM, NSA_HEAD_DIM), (CMP_LEN * NSA_HEAD_DIM) ** -0.5),
        'w_cmp_v': nrm(ks[9], (L, CMP_LEN * NSA_HEAD_DIM, NSA_HEAD_DIM), (CMP_LEN * NSA_HEAD_DIM) ** -0.5),
        'q_gain': 1.0 + nrm(ks[10], (L, NSA_HEAD_DIM), 0.02),
        'k_gain': 1.0 + nrm(ks[11], (L, 3, NSA_HEAD_DIM), 0.02),
        'conv_w': nrm(ks[12], (L, CONV_W, LRU_WIDTH), CONV_W ** -0.5),
        'conv_b': nrm(ks[13], (L, LRU_WIDTH), 0.02),
        'w_rg': nrm(ks[14], (L, LRU_BLOCKS, LRU_BLOCK_W, LRU_BLOCK_W), LRU_BLOCK_W ** -0.5),
        'b_rg': nrm(ks[15], (L, LRU_BLOCKS, LRU_BLOCK_W), 0.02),
        'w_ig': nrm(ks[16], (L, LRU_BLOCKS, LRU_BLOCK_W, LRU_BLOCK_W), LRU_BLOCK_W ** -0.5),
        'b_ig': nrm(ks[17], (L, LRU_BLOCKS, LRU_BLOCK_W), 0.02),
        'lru_lambda': jnp.log(a0) - jnp.log1p(-a0),
        'g_out_nsa': 1.0 + nrm(ks[19], (L, NSA_WIDTH), 0.02),
        'g_out_lru': 1.0 + nrm(ks[20], (L, LRU_WIDTH), 0.02),
        'w_out': nrm(ks[21], (L, D_MODEL, D_MODEL), D_MODEL ** -0.5),
        'g_norm2': 1.0 + nrm(ks[22], (L, D_MODEL), 0.02),
        'w_router': nrm(ks[23], (L, D_MODEL, N_EXPERTS), D_MODEL ** -0.5),
        'b_router': nrm(ks[24], (L, N_EXPERTS), 0.01),
        'w_e1': nrm(ks[25], (L, N_EXPERTS, D_MODEL, 2 * D_FF_EXPERT), D_MODEL ** -0.5),
        'b_e1': nrm(ks[26], (L, N_EXPERTS, 2 * D_FF_EXPERT), 0.02),
        'w_e2': nrm(ks[27], (L, N_EXPERTS, D_FF_EXPERT, D_MODEL), D_FF_EXPERT ** -0.5),
        'b_e2': nrm(ks[28], (L, N_EXPERTS, D_MODEL), 0.02),
    }


def reference(x, c, w_ada, b_ada, g_norm1, w_in, pe_cmp_k, pe_cmp_v, w_cmp_k, w_cmp_v, q_gain, k_gain,
              conv_w, conv_b, w_rg, b_rg, w_ig, b_ig, lru_lambda, g_out_nsa, g_out_lru, w_out,
              g_norm2, w_router, b_router, w_e1, b_e1, w_e2, b_e2):
    for l in range(DEPTH):
        mod = jax.nn.silu(c) @ w_ada[l] + b_ada[l]
        x = hybrid_layer(x, mod, g_norm1[l], w_in[l], pe_cmp_k[l], pe_cmp_v[l], w_cmp_k[l], w_cmp_v[l],
                         q_gain[l], k_gain[l], conv_w[l], conv_b[l], w_rg[l], b_rg[l], w_ig[l], b_ig[l],
                         lru_lambda[l], g_out_nsa[l], g_out_lru[l], w_out[l], g_norm2[l],
                         w_router[l], b_router[l], w_e1[l], b_e1[l], w_e2[l], b_e2[l])
    return x
```

```python
import functools

import jax
import jax.numpy as jnp
from jax import lax
from jax.experimental import pallas as pl
from jax.experimental.pallas import tpu as pltpu

f32 = jnp.float32
bf16 = jnp.bfloat16
i32 = jnp.int32

D_MODEL = 2048
N_HEADS = 16
HEAD_DIM = 64
KV_GROUPS = 4
HEADS_PER_GROUP = N_HEADS // KV_GROUPS
NSA_WIDTH = N_HEADS * HEAD_DIM
KV_WIDTH = KV_GROUPS * HEAD_DIM
CMP_LEN = 32
CMP_STRIDE = 16
SEL_LEN = 64
N_SEL = 8
WINDOW = 512
LRU_WIDTH = D_MODEL - NSA_WIDTH
LRU_BLOCKS = 16
CONV_W = 4
RG_C = 8.0
N_EXPERTS = 32
TOP_K = 4
D_FF = D_MODEL
SWIGLU_LIMIT = 7.0
SWIGLU_ALPHA = 1.702
NORM_EPS = 1e-6
NEG_INF = -1e30
SEL_FORCE = 1e30
GATE_PAD = 128
IN_PAD = NSA_WIDTH + 6 * KV_WIDTH + GATE_PAD + 2 * LRU_WIDTH
LANES = 128
MIB = 1024 * 1024

TM_PROJ = 256
TQ = 128
TS_LRU = 256
TM_ROUTE = 512
MOE_BLK = 256
TM_DISP = 256
TM_COMB = 128
TF_UP = 512
TN_DOWN = 1024


def _cparams(vmem_mib, n_axes):
    return pltpu.CompilerParams(
        vmem_limit_bytes=int(vmem_mib * MIB),
        dimension_semantics=("arbitrary",) * n_axes,
    )


def _dot(a, b):
    return jnp.dot(a, b, preferred_element_type=f32)


def _dot_nt(a, b):
    return lax.dot_general(a, b, (((1,), (1,)), ((), ())), preferred_element_type=f32)


def _split_bf16(x):
    hi = x.astype(bf16)
    lo = (x - hi.astype(f32)).astype(bf16)
    return hi, lo


def _group_meansq(x, group):
    w = x.shape[1]
    r = lax.broadcasted_iota(i32, (w, w), 0) // group
    c = lax.broadcasted_iota(i32, (w, w), 1) // group
    ones_bd = jnp.where(r == c, 1.0, 0.0).astype(bf16)
    hi, lo = _split_bf16(x * x)
    return (_dot(hi, ones_bd) + _dot(lo, ones_bd)) * (1.0 / group)


def _rms_rows(x, gain):
    ms = jnp.mean(x * x, axis=-1, keepdims=True)
    return x * lax.rsqrt(ms + NORM_EPS) * gain


def _masked_softmax(s, m):
    sm = jnp.where(m, s, NEG_INF)
    mx = jnp.max(sm, axis=-1, keepdims=True)
    e = jnp.where(m, jnp.exp(sm - mx), 0.0)
    den = jnp.sum(e, axis=-1, keepdims=True)
    inv = jnp.where(den > 0.0, 1.0 / den, 0.0)
    return e * inv


def _ada_kernel(c_ref, w_ref, b_ref, o_ref):
    c = c_ref[...]
    sc = c * jax.nn.sigmoid(c)
    o_ref[...] = _dot(sc.astype(bf16), w_ref[...].astype(bf16)) + b_ref[...]


def _ada_mod(c, w_ada, b_ada):
    bsz, d = c.shape
    n = w_ada.shape[1]
    tn = 1024
    return pl.pallas_call(
        _ada_kernel,
        grid=(n // tn,),
        in_specs=[
            pl.BlockSpec((bsz, d), lambda j: (0, 0)),
            pl.BlockSpec((d, tn), lambda j: (0, j)),
            pl.BlockSpec((1, tn), lambda j: (0, j)),
        ],
        out_specs=pl.BlockSpec((bsz, tn), lambda j: (0, j)),
        out_shape=jax.ShapeDtypeStruct((bsz, n), f32),
        compiler_params=_cparams(40, 1),
        name="ada_mod",
    )(c, w_ada, b_ada.reshape(1, n))


def _inproj_kernel(x_ref, g_ref, mod_ref, w_ref, zq_ref, zkv_ref, zgl_ref, zx_ref):
    x = x_ref[...]
    y = _rms_rows(x, g_ref[...])
    h = y * (1.0 + mod_ref[1:2, :]) + mod_ref[0:1, :]
    z = _dot(h.astype(bf16), w_ref[...])
    o = 0
    for ref in (zq_ref, zkv_ref, zgl_ref, zx_ref):
        wdt = ref.shape[1]
        ref[...] = z[:, o:o + wdt]
        o += wdt


def _inproj(xf, g1, mod3, w_pad, seq):
    n, d = xf.shape
    tm = TM_PROJ
    tpb = seq // tm
    widths = (NSA_WIDTH, 6 * KV_WIDTH, GATE_PAD, 2 * LRU_WIDTH)
    return pl.pallas_call(
        _inproj_kernel,
        grid=(n // tm,),
        in_specs=[
            pl.BlockSpec((tm, d), lambda i: (i, 0)),
            pl.BlockSpec((1, d), lambda i: (0, 0)),
            pl.BlockSpec((None, 6, d), lambda i: (i // tpb, 0, 0)),
            pl.BlockSpec((d, IN_PAD), lambda i: (0, 0), pipeline_mode=pl.Buffered(1)),
        ],
        out_specs=[pl.BlockSpec((tm, w), lambda i: (i, 0)) for w in widths],
        out_shape=[jax.ShapeDtypeStruct((n, w), f32) for w in widths],
        compiler_params=_cparams(56, 1),
        name="in_proj",
    )(xf, g1.reshape(1, d), mod3, w_pad)


def _prep_kernel(zq_ref, zkv_ref, zgl_ref, qg_ref, kg_ref, qn_ref, ks_ref, vs_ref, kw_ref, vw_ref, gate_ref):
    gw = HEADS_PER_GROUP * HEAD_DIM
    qg = qg_ref[...]
    for g in range(KV_GROUPS):
        xg = zq_ref[:, g * gw:(g + 1) * gw]
        ms = _group_meansq(xg, HEAD_DIM)
        qn = xg * lax.rsqrt(ms + NORM_EPS) * qg * (HEAD_DIM ** -0.5)
        qn_ref[:, g * gw:(g + 1) * gw] = qn.astype(bf16)

    def norm_k(col, row):
        xk = zkv_ref[:, col * KV_WIDTH:(col + 1) * KV_WIDTH]
        ms = _group_meansq(xk, HEAD_DIM)
        return xk * lax.rsqrt(ms + NORM_EPS) * kg_ref[row:row + 1, :]

    ksn = norm_k(2, 1)
    kwn = norm_k(4, 2)
    vs = zkv_ref[:, 3 * KV_WIDTH:4 * KV_WIDTH]
    vw = zkv_ref[:, 5 * KV_WIDTH:6 * KV_WIDTH]
    gl = jax.nn.sigmoid(zgl_ref[...])
    ng = 3 * HEADS_PER_GROUP
    for g in range(KV_GROUPS):
        sl = slice(g * HEAD_DIM, (g + 1) * HEAD_DIM)
        ks_ref[g] = ksn[:, sl].astype(bf16)
        vs_ref[g] = vs[:, sl].astype(bf16)
        kw_ref[g] = kwn[:, sl].astype(bf16)
        vw_ref[g] = vw[:, sl].astype(bf16)
        gate_ref[g] = gl[:, g * ng:(g + 1) * ng]


def _prep(zq, zkv, zgl, q_gain, k_gain, bsz, seq):
    n = zq.shape[0]
    tm = TM_PROJ
    tpb = seq // tm
    qg = jnp.tile(q_gain.reshape(1, HEAD_DIM), (1, HEADS_PER_GROUP))
    kg = jnp.tile(k_gain.reshape(3, HEAD_DIM), (1, KV_GROUPS))
    hm = lambda w: pl.BlockSpec((None, KV_GROUPS, tm, w), lambda i: (i // tpb, 0, i % tpb, 0))
    hshape = lambda w, dt: jax.ShapeDtypeStruct((bsz, KV_GROUPS, seq, w), dt)
    ng = 3 * HEADS_PER_GROUP
    return pl.pallas_call(
        _prep_kernel,
        grid=(n // tm,),
        in_specs=[
            pl.BlockSpec((tm, NSA_WIDTH), lambda i: (i, 0)),
            pl.BlockSpec((tm, 6 * KV_WIDTH), lambda i: (i, 0)),
            pl.BlockSpec((tm, GATE_PAD), lambda i: (i, 0)),
            pl.BlockSpec((1, HEADS_PER_GROUP * HEAD_DIM), lambda i: (0, 0)),
            pl.BlockSpec((3, KV_WIDTH), lambda i: (0, 0)),
        ],
        out_specs=[
            pl.BlockSpec((tm, NSA_WIDTH), lambda i: (i, 0)),
            hm(HEAD_DIM), hm(HEAD_DIM), hm(HEAD_DIM), hm(HEAD_DIM), hm(ng),
        ],
        out_shape=[
            jax.ShapeDtypeStruct((n, NSA_WIDTH), bf16),
            hshape(HEAD_DIM, bf16), hshape(HEAD_DIM, bf16), hshape(HEAD_DIM, bf16), hshape(HEAD_DIM, bf16),
            hshape(ng, f32),
        ],
        compiler_params=_cparams(32, 1),
        name="nsa_prep",
    )(zq, zkv, zgl, qg, kg)


def _compress_kernel(fk_ref, fv_ref, pek_ref, pev_ref, wk_ref, wv_ref, kg_ref, kc_ref, vc_ref):
    kc = _dot((fk_ref[...] + pek_ref[...]).astype(bf16), wk_ref[...])
    vc = _dot((fv_ref[...] + pev_ref[...]).astype(bf16), wv_ref[...])
    kc_ref[...] = _rms_rows(kc, kg_ref[...]).astype(bf16)
    vc_ref[...] = vc.astype(bf16)


def _flat_blocks(z, bsz, seq):
    nsub = seq // CMP_STRIDE
    sub = z.reshape(bsz, nsub, CMP_STRIDE, KV_GROUPS, HEAD_DIM)
    nxt = jnp.concatenate([sub[:, 1:], jnp.zeros_like(sub[:, :1])], axis=1)
    blocks = jnp.concatenate([sub, nxt], axis=2)
    return blocks.transpose(0, 3, 1, 2, 4).reshape(bsz, KV_GROUPS, nsub, CMP_LEN * HEAD_DIM)


def _compress(zkv, pe_k, pe_v, w_ck, w_cv, k_gain0, bsz, seq):
    nsub = seq // CMP_STRIDE
    fk = _flat_blocks(zkv[:, 0:KV_WIDTH], bsz, seq)
    fv = _flat_blocks(zkv[:, KV_WIDTH:2 * KV_WIDTH], bsz, seq)
    kdim = CMP_LEN * HEAD_DIM
    blk = pl.BlockSpec((None, None, nsub, kdim), lambda b, g: (b, g, 0, 0))
    cst = lambda r, c: pl.BlockSpec((r, c), lambda b, g: (0, 0))
    oblk = pl.BlockSpec((None, None, nsub, HEAD_DIM), lambda b, g: (b, g, 0, 0))
    oshape = jax.ShapeDtypeStruct((bsz, KV_GROUPS, nsub, HEAD_DIM), bf16)
    return pl.pallas_call(
        _compress_kernel,
        grid=(bsz, KV_GROUPS),
        in_specs=[blk, blk, cst(1, kdim), cst(1, kdim), cst(kdim, HEAD_DIM), cst(kdim, HEAD_DIM), cst(1, HEAD_DIM)],
        out_specs=[oblk, oblk],
        out_shape=[oshape, oshape],
        compiler_params=_cparams(32, 2),
        name="nsa_compress",
    )(fk, fv, pe_k.reshape(1, kdim), pe_v.reshape(1, kdim), w_ck.astype(bf16), w_cv.astype(bf16),
      k_gain0.reshape(1, HEAD_DIM))


def _attn_kernel(q_ref, kc_ref, vc_ref, ks_ref, vs_ref, kw_ref, vw_ref, gate_ref, o_ref, *, seq):
    t = pl.program_id(2)
    hg = HEADS_PER_GROUP
    q = q_ref[...]
    qs = jnp.concatenate([q[:, h * HEAD_DIM:(h + 1) * HEAD_DIM] for h in range(hg)], axis=0)
    tpos = t * TQ + lax.broadcasted_iota(i32, (TQ, 1), 0)
    tpos4 = jnp.concatenate([tpos] * hg, axis=0)

    ncmp = kc_ref.shape[0]
    nblk = seq // SEL_LEN
    s_c = _dot_nt(qs, kc_ref[...])
    cstart = lax.broadcasted_iota(i32, (1, ncmp), 1) * CMP_STRIDE
    p_c = _masked_softmax(s_c, (cstart + (CMP_LEN - 1)) <= tpos4).astype(bf16)
    o_c = _dot(p_c, vc_ref[...])

    cs = lax.broadcasted_iota(i32, (ncmp, nblk), 0) * CMP_STRIDE
    ss = lax.broadcasted_iota(i32, (ncmp, nblk), 1) * SEL_LEN
    overlap = jnp.where((cs < ss + SEL_LEN) & (cs + CMP_LEN > ss), 1.0, 0.0).astype(bf16)
    imp4 = _dot(p_c, overlap)
    imp = imp4[0:TQ]
    for h in range(1, hg):
        imp = imp + imp4[h * TQ:(h + 1) * TQ]
    j = lax.broadcasted_iota(i32, (1, nblk), 1)
    qblk = jnp.right_shift(tpos, 6)
    forced = (j == 0) | (j == qblk) | (j == qblk - 1)
    impf = jnp.where(forced, SEL_FORCE, jnp.where(j <= qblk, imp, -SEL_FORCE))
    beaten = jnp.zeros((TQ, nblk), i32)
    for i in range(nblk):
        ci = impf[:, i:i + 1]
        beats = (ci > impf) | ((ci == impf) & (j > i))
        beaten = beaten + jnp.where(beats, 1, 0)
    sel = jnp.where(beaten < min(N_SEL, nblk), 1.0, 0.0).astype(bf16)
    eb = lax.broadcasted_iota(i32, (nblk, seq), 0)
    ep = jnp.right_shift(lax.broadcasted_iota(i32, (nblk, seq), 1), 6)
    selk = _dot(sel, jnp.where(eb == ep, 1.0, 0.0).astype(bf16))
    kpos = lax.broadcasted_iota(i32, (1, seq), 1)
    m_s = (selk > 0.5) & (kpos <= tpos)

    s_s = _dot_nt(qs, ks_ref[...])
    p_s = jnp.concatenate(
        [_masked_softmax(s_s[h * TQ:(h + 1) * TQ], m_s).astype(bf16) for h in range(hg)], axis=0)
    o_s = _dot(p_s, vs_ref[...])

    wk = WINDOW + TQ
    start = pl.multiple_of(jnp.maximum(t * TQ - WINDOW, 0), TQ)
    s_w = _dot_nt(qs, kw_ref[pl.ds(start, wk), :])
    wpos = start + lax.broadcasted_iota(i32, (1, wk), 1)
    m_w = (wpos <= tpos4) & (wpos > tpos4 - WINDOW)
    o_w = _dot(_masked_softmax(s_w, m_w).astype(bf16), vw_ref[pl.ds(start, wk), :])

    gts = gate_ref[...]
    outs = []
    for h in range(hg):
        rows = slice(h * TQ, (h + 1) * TQ)
        outs.append(gts[:, 3 * h:3 * h + 1] * o_c[rows] + gts[:, 3 * h + 1:3 * h + 2] * o_s[rows]
                    + gts[:, 3 * h + 2:3 * h + 3] * o_w[rows])
    o_ref[...] = jnp.concatenate(outs, axis=1)


def _attention(qn, kc, vc, ks, vs, kw, vw, gates, bsz, seq):
    n = qn.shape[0]
    gw = HEADS_PER_GROUP * HEAD_DIM
    ntq = seq // TQ
    nsub = seq // CMP_STRIDE
    ng = 3 * HEADS_PER_GROUP
    qblk = pl.BlockSpec((TQ, gw), lambda b, g, t: (b * ntq + t, g))
    full = lambda r, w: pl.BlockSpec((None, None, r, w), lambda b, g, t: (b, g, 0, 0))
    return pl.pallas_call(
        functools.partial(_attn_kernel, seq=seq),
        grid=(bsz, KV_GROUPS, ntq),
        in_specs=[
            qblk, full(nsub, HEAD_DIM), full(nsub, HEAD_DIM),
            full(seq, HEAD_DIM), full(seq, HEAD_DIM), full(seq, HEAD_DIM), full(seq, HEAD_DIM),
            pl.BlockSpec((None, None, TQ, ng), lambda b, g, t: (b, g, t, 0)),
        ],
        out_specs=qblk,
        out_shape=jax.ShapeDtypeStruct((n, NSA_WIDTH), f32),
        compiler_params=_cparams(48, 3),
        name="nsa_attention",
    )(qn, kc, vc, ks, vs, kw, vw, gates)


def _lru_kernel(xr_ref, xg_ref, cw_ref, cb_ref, wr_ref, br_ref, wi_ref, bi_ref, lam_ref, o_ref,
                xbuf, hstate, abuf, ubuf):
    ts = xr_ref.shape[0]
    s = pl.program_id(1)

    @pl.when(s == 0)
    def _():
        xbuf[0:8, :] = jnp.zeros((8, LRU_WIDTH), f32)
        hstate[...] = jnp.zeros_like(hstate)

    xbuf[8:8 + ts, :] = xr_ref[...]
    acc = xbuf[5:5 + ts, :] * cw_ref[0:1, :]
    for k in range(1, CONV_W):
        acc = acc + xbuf[5 + k:5 + k + ts, :] * cw_ref[k:k + 1, :]
    xc = cb_ref[...] + acc
    xbuf[0:8, :] = xbuf[ts:ts + 8, :]

    xcb = xc.astype(bf16)
    r = jax.nn.sigmoid(_dot(xcb, wr_ref[...]) + br_ref[...])
    ig = jax.nn.sigmoid(_dot(xcb, wi_ref[...]) + bi_ref[...])
    nl = -lam_ref[...]
    softplus = jnp.maximum(nl, 0.0) + jnp.log1p(jnp.exp(-jnp.abs(nl)))
    log_a = (-RG_C * softplus) * r
    abuf[...] = jnp.exp(log_a)
    th = jnp.tanh(log_a)
    ubuf[...] = jnp.sqrt(-2.0 * th / (1.0 - th)) * (ig * xc)

    def step(jb, h):
        base = pl.multiple_of(jb * 8, 8)
        a8 = abuf[pl.ds(base, 8), :]
        u8 = ubuf[pl.ds(base, 8), :]
        rows = []
        for rr in range(8):
            h = a8[rr:rr + 1, :] * h + u8[rr:rr + 1, :]
            rows.append(h)
        ubuf[pl.ds(base, 8), :] = jnp.concatenate(rows, axis=0)
        return h

    hstate[0:1, :] = lax.fori_loop(0, ts // 8, step, hstate[0:1, :])
    xg = xg_ref[...]
    gelu = 0.5 * xg * (1.0 + jnp.tanh(0.7978845608028654 * (xg + 0.044715 * (xg * xg * xg))))
    o_ref[...] = ubuf[...] * gelu


def _block_diag(w):
    nb, bw, _ = w.shape
    eye = jnp.eye(nb, dtype=w.dtype)
    return (w[:, :, None, :] * eye[:, None, :, None]).reshape(nb * bw, nb * bw)


def _lru(zx, conv_w, conv_b, w_rg, b_rg, w_ig, b_ig, lam, bsz, seq):
    n = zx.shape[0]
    ts = TS_LRU
    tpb = seq // ts
    w = LRU_WIDTH
    row = lambda v: v.reshape(1, w)
    cst = lambda r, c: pl.BlockSpec((r, c), lambda b, s: (0, 0))
    return pl.pallas_call(
        _lru_kernel,
        grid=(bsz, tpb),
        in_specs=[
            pl.BlockSpec((ts, w), lambda b, s: (b * tpb + s, 0)),
            pl.BlockSpec((ts, w), lambda b, s: (b * tpb + s, 1)),
            cst(CONV_W, w), cst(1, w), cst(w, w), cst(1, w), cst(w, w), cst(1, w), cst(1, w),
        ],
        out_specs=pl.BlockSpec((ts, w), lambda b, s: (b * tpb + s, 0)),
        out_shape=jax.ShapeDtypeStruct((n, w), f32),
        scratch_shapes=[
            pltpu.VMEM((ts + 8, w), f32), pltpu.VMEM((8, w), f32),
            pltpu.VMEM((ts, w), f32), pltpu.VMEM((ts, w), f32),
        ],
        compiler_params=_cparams(40, 2),
        name="rglru",
    )(zx, zx, conv_w, row(conv_b), _block_diag(w_rg).astype(bf16), row(b_rg),
      _block_diag(w_ig).astype(bf16), row(b_ig), row(lam))


def _outproj_kernel(on_ref, ol_ref, x_ref, mod_ref, gn_ref, gl_ref, w_ref, g2_ref, wrh_ref, wrl_ref, br_ref,
                    x1_ref, hp_ref, te_ref, tg_ref):
    nn = _rms_rows(on_ref[...], gn_ref[...])
    nl = _rms_rows(ol_ref[...], gl_ref[...])
    hcat = jnp.concatenate([nn, nl], axis=1).astype(bf16)
    mix = _dot(hcat, w_ref[...])
    x1 = x_ref[...] + mod_ref[2:3, :] * mix
    x1_ref[...] = x1
    h2 = _rms_rows(x1, g2_ref[...]) * (1.0 + mod_ref[4:5, :]) + mod_ref[3:4, :]

    half = D_MODEL // 2
    hb = h2.astype(bf16).astype(f32)
    hi_bits = lax.bitcast_convert_type(hb[:, :half], i32)
    lo_bits = lax.shift_right_logical(lax.bitcast_convert_type(hb[:, half:], i32), 16)
    hp_ref[...] = hi_bits | lo_bits

    hh, hl = _split_bf16(h2)
    logits = _dot(hh, wrh_ref[...]) + _dot(hl, wrh_ref[...]) + _dot(hh, wrl_ref[...]) + br_ref[...]
    lane = lax.broadcasted_iota(i32, (1, LANES), 1)
    lane_f = lane.astype(f32)
    vals, idxs = [], []
    cur = logits
    for _ in range(TOP_K):
        m = jnp.max(cur, axis=-1, keepdims=True)
        idx = jnp.min(jnp.where(cur == m, lane_f, float(LANES)), axis=-1, keepdims=True).astype(i32)
        vals.append(m)
        idxs.append(idx)
        cur = jnp.where(lane == idx, -3e38, cur)
    es = [jnp.exp(v - vals[0]) for v in vals]
    den = es[0]
    for e in es[1:]:
        den = den + e
    inv = 1.0 / den
    te = jnp.full(logits.shape, -1, i32)
    tg = jnp.zeros(logits.shape, f32)
    for k in range(TOP_K):
        te = jnp.where(lane == k, idxs[k], te)
        tg = jnp.where(lane == k, es[k] * inv, tg)
    te_ref[...] = te
    tg_ref[...] = tg


def _outproj(o_nsa, o_lru, xf, mod3, g_out_nsa, g_out_lru, w_out, g_norm2, w_router, b_router, seq):
    n, d = xf.shape
    tm = TM_PROJ
    tpb = seq // tm
    wr = jnp.zeros((d, LANES), f32).at[:, :N_EXPERTS].set(w_router)
    wrh = wr.astype(bf16)
    wrl = (wr - wrh.astype(f32)).astype(bf16)
    br = jnp.full((1, LANES), NEG_INF, f32).at[0, :N_EXPERTS].set(b_router)
    rows = lambda w: pl.BlockSpec((tm, w), lambda i: (i, 0))
    cst = lambda r, c: pl.BlockSpec((r, c), lambda i: (0, 0))
    return pl.pallas_call(
        _outproj_kernel,
        grid=(n // tm,),
        in_specs=[
            rows(NSA_WIDTH), rows(LRU_WIDTH), rows(d),
            pl.BlockSpec((None, 6, d), lambda i: (i // tpb, 0, 0)),
            cst(1, NSA_WIDTH), cst(1, LRU_WIDTH), cst(d, d), cst(1, d), cst(d, LANES), cst(d, LANES), cst(1, LANES),
        ],
        out_specs=[rows(d), rows(d // 2), rows(LANES), rows(LANES)],
        out_shape=[
            jax.ShapeDtypeStruct((n, d), f32), jax.ShapeDtypeStruct((n, d // 2), i32),
            jax.ShapeDtypeStruct((n, LANES), i32), jax.ShapeDtypeStruct((n, LANES), f32),
        ],
        compiler_params=_cparams(48, 1),
        name="out_proj_router",
    )(o_nsa, o_lru, xf, mod3, g_out_nsa.reshape(1, -1), g_out_lru.reshape(1, -1), w_out.astype(bf16),
      g_norm2.reshape(1, d), wrh, wrl, br)


def _rank_kernel(te_ref, rank_ref, cnt_ref, carry):
    i = pl.program_id(0)
    tm = te_ref.shape[0]

    @pl.when(i == 0)
    def _():
        carry[...] = jnp.zeros_like(carry)

    te = te_ref[...]
    lane = lax.broadcasted_iota(i32, (1, LANES), 1)
    hits = [te[:, k:k + 1] == lane for k in range(TOP_K)]
    onehot = jnp.zeros((tm, LANES), f32)
    for h in hits:
        onehot = onehot + jnp.where(h, 1.0, 0.0)
    r = lax.broadcasted_iota(i32, (tm, tm), 0)
    c = lax.broadcasted_iota(i32, (tm, tm), 1)
    lower = jnp.where(c < r, 1.0, 0.0).astype(bf16)
    prefix = _dot(lower, onehot.astype(bf16)) + carry[0:1, :]
    rank = jnp.zeros((tm, LANES), f32)
    for k in range(TOP_K):
        rk = jnp.sum(jnp.where(hits[k], prefix, 0.0), axis=-1, keepdims=True)
        rank = jnp.where(lane == k, rk, rank)
    rank_ref[...] = rank.astype(i32)
    carry[0:1, :] = carry[0:1, :] + jnp.sum(onehot, axis=0, keepdims=True)
    cnt_ref[...] = carry[...]


def _ranks(te_pad):
    n = te_pad.shape[0]
    tm = TM_ROUTE
    return pl.pallas_call(
        _rank_kernel,
        grid=(n // tm,),
        in_specs=[pl.BlockSpec((tm, LANES), lambda i: (i, 0))],
        out_specs=[pl.BlockSpec((tm, LANES), lambda i: (i, 0)), pl.BlockSpec((8, LANES), lambda i: (0, 0))],
        out_shape=[jax.ShapeDtypeStruct((n, LANES), i32), jax.ShapeDtypeStruct((8, LANES), f32)],
        scratch_shapes=[pltpu.VMEM((8, LANES), f32)],
        compiler_params=_cparams(32, 1),
        name="route_ranks",
    )(te_pad)


def _slot_kernel(te_ref, rank_ref, pstart_ref, dest_ref):
    te = te_ref[...]
    lane = lax.broadcasted_iota(i32, (1, LANES), 1)
    ps = pstart_ref[...].astype(f32)
    dest = jnp.zeros(te.shape, i32)
    for k in range(TOP_K):
        base = jnp.sum(jnp.where(te[:, k:k + 1] == lane, ps, 0.0), axis=-1, keepdims=True)
        dest = jnp.where(lane == k, base.astype(i32), dest)
    dest_ref[...] = dest + rank_ref[...]


def _slots(te_pad, rank_pad, pstart_row):
    n = te_pad.shape[0]
    tm = TM_ROUTE
    blk = pl.BlockSpec((tm, LANES), lambda i: (i, 0))
    return pl.pallas_call(
        _slot_kernel,
        grid=(n // tm,),
        in_specs=[blk, blk, pl.BlockSpec((1, LANES), lambda i: (0, 0))],
        out_specs=blk,
        out_shape=jax.ShapeDtypeStruct((n, LANES), i32),
        compiler_params=_cparams(32, 1),
        name="route_slots",
    )(te_pad, rank_pad, pstart_row)


def _row_copy(src, src_row, dst, dst_row, sem):
    return pltpu.make_async_copy(src.at[pl.ds(src_row, 1)], dst.at[pl.ds(dst_row, 1)], sem)


def _dispatch_kernel(dest_ref, h_ref, xs_in_ref, xs_ref, sem):
    del xs_in_ref
    tm = h_ref.shape[0]

    def issue(i, c):
        for k in range(TOP_K):
            _row_copy(h_ref, i, xs_ref, dest_ref[0, i * TOP_K + k], sem).start()
        return c

    lax.fori_loop(0, tm, issue, 0)

    def drain(i, c):
        for k in range(TOP_K):
            _row_copy(h_ref, 0, xs_ref, 0, sem).wait()
        return c

    lax.fori_loop(0, tm, drain, 0)


def _dispatch(dest_tiles, hp, n_slots):
    n, w = hp.shape
    tm = TM_DISP
    xs0 = jnp.zeros((n_slots, w), i32)
    return pl.pallas_call(
        _dispatch_kernel,
        grid=(n // tm,),
        in_specs=[
            pl.BlockSpec((None, 1, tm * TOP_K), lambda i: (i, 0, 0), memory_space=pltpu.SMEM),
            pl.BlockSpec((tm, w), lambda i: (i, 0)),
            pl.BlockSpec(memory_space=pl.ANY),
        ],
        out_specs=pl.BlockSpec(memory_space=pl.ANY),
        out_shape=jax.ShapeDtypeStruct((n_slots, w), i32),
        scratch_shapes=[pltpu.SemaphoreType.DMA(())],
        input_output_aliases={2: 0},
        compiler_params=_cparams(32, 1),
        name="moe_dispatch",
    )(dest_tiles, hp, xs0)


def _unpack_rows(xp):
    hi = lax.bitcast_convert_type(xp & jnp.int32(-65536), f32)
    lo = lax.bitcast_convert_type(lax.shift_left(xp, 16), f32)
    return jnp.concatenate([hi, lo], axis=1).astype(bf16)


def _up_kernel(be_ref, nb_ref, xs_ref, wg_ref, wl_ref, bg_ref, bl_ref, act_ref):
    del be_ref

    @pl.when(pl.program_id(1) < nb_ref[0])
    def _():
        x = _unpack_rows(xs_ref[...])
        ug = _dot(x, wg_ref[...].astype(bf16)) + bg_ref[...]
        ul = _dot(x, wl_ref[...].astype(bf16)) + bl_ref[...]
        ug = jnp.minimum(ug, SWIGLU_LIMIT)
        ul = jnp.clip(ul, -SWIGLU_LIMIT, SWIGLU_LIMIT)
        act_ref[...] = (ug * jax.nn.sigmoid(SWIGLU_ALPHA * ug) * (ul + 1.0)).astype(bf16)

    @pl.when(pl.program_id(1) >= nb_ref[0])
    def _():
        act_ref[...] = jnp.zeros_like(act_ref)


def _expert_up(blk_e, n_used, xs, w_e1, b_e1):
    n_slots, w = xs.shape
    nblk = n_slots // MOE_BLK
    tf = TF_UP
    ncol = D_FF // tf
    d = D_MODEL
    grid_spec = pltpu.PrefetchScalarGridSpec(
        num_scalar_prefetch=2,
        grid=(ncol, nblk),
        in_specs=[
            pl.BlockSpec((MOE_BLK, w), lambda n, i, be, nb: (i, 0)),
            pl.BlockSpec((None, d, tf), lambda n, i, be, nb: (be[i], 0, n)),
            pl.BlockSpec((None, d, tf), lambda n, i, be, nb: (be[i], 0, ncol + n)),
            pl.BlockSpec((None, 1, tf), lambda n, i, be, nb: (be[i], 0, n)),
            pl.BlockSpec((None, 1, tf), lambda n, i, be, nb: (be[i], 0, ncol + n)),
        ],
        out_specs=pl.BlockSpec((MOE_BLK, tf), lambda n, i, be, nb: (i, n)),
    )
    return pl.pallas_call(
        _up_kernel,
        grid_spec=grid_spec,
        out_shape=jax.ShapeDtypeStruct((n_slots, D_FF), bf16),
        compiler_params=_cparams(48, 2),
        name="moe_up",
    )(blk_e, n_used, xs, w_e1, w_e1, b_e1.reshape(N_EXPERTS, 1, 2 * D_FF), b_e1.reshape(N_EXPERTS, 1, 2 * D_FF))


def _down_kernel(be_ref, nb_ref, act_ref, w_ref, b_ref, y_ref):
    del be_ref

    @pl.when(pl.program_id(1) < nb_ref[0])
    def _():
        y_ref[...] = _dot(act_ref[...], w_ref[...].astype(bf16)) + b_ref[...]

    @pl.when(pl.program_id(1) >= nb_ref[0])
    def _():
        y_ref[...] = jnp.zeros_like(y_ref)


def _expert_down(blk_e, n_used, act, w_e2, b_e2):
    n_slots = act.shape[0]
    nblk = n_slots // MOE_BLK
    tn = TN_DOWN
    ncol = D_MODEL // tn
    grid_spec = pltpu.PrefetchScalarGridSpec(
        num_scalar_prefetch=2,
        grid=(ncol, nblk),
        in_specs=[
            pl.BlockSpec((MOE_BLK, D_FF), lambda n, i, be, nb: (i, 0)),
            pl.BlockSpec((None, D_FF, tn), lambda n, i, be, nb: (be[i], 0, n)),
            pl.BlockSpec((None, 1, tn), lambda n, i, be, nb: (be[i], 0, n)),
        ],
        out_specs=pl.BlockSpec((MOE_BLK, tn), lambda n, i, be, nb: (i, n)),
    )
    return pl.pallas_call(
        _down_kernel,
        grid_spec=grid_spec,
        out_shape=jax.ShapeDtypeStruct((n_slots, D_MODEL), f32),
        compiler_params=_cparams(48, 2),
        name="moe_down",
    )(blk_e, n_used, act, w_e2, b_e2.reshape(N_EXPERTS, 1, D_MODEL))


def _combine_kernel(dest_ref, y_ref, x1_ref, tg_ref, mod_ref, o_ref, buf, sem):
    tm = x1_ref.shape[0]

    def issue(i, c):
        for k in range(TOP_K):
            _row_copy(y_ref, dest_ref[0, i * TOP_K + k], buf.at[k], i, sem).start()
        return c

    lax.fori_loop(0, tm, issue, 0)

    def drain(i, c):
        for k in range(TOP_K):
            _row_copy(y_ref, 0, buf.at[k], 0, sem).wait()
        return c

    lax.fori_loop(0, tm, drain, 0)
    tg = tg_ref[...]
    acc = tg[:, 0:1] * buf[0]
    for k in range(1, TOP_K):
        acc = acc + tg[:, k:k + 1] * buf[k]
    o_ref[...] = x1_ref[...] + mod_ref[5:6, :] * acc


def _combine(dest_tiles, y, x1, tg_pad, mod3, seq):
    n, d = x1.shape
    tm = TM_COMB
    tpb = seq // tm
    return pl.pallas_call(
        _combine_kernel,
        grid=(n // tm,),
        in_specs=[
            pl.BlockSpec((None, 1, tm * TOP_K), lambda i: (i, 0, 0), memory_space=pltpu.SMEM),
            pl.BlockSpec(memory_space=pl.ANY),
            pl.BlockSpec((tm, d), lambda i: (i, 0)),
            pl.BlockSpec((tm, LANES), lambda i: (i, 0)),
            pl.BlockSpec((None, 6, d), lambda i: (i // tpb, 0, 0)),
        ],
        out_specs=pl.BlockSpec((tm, d), lambda i: (i, 0)),
        out_shape=jax.ShapeDtypeStruct((n, d), f32),
        scratch_shapes=[pltpu.VMEM((TOP_K, tm, d), f32), pltpu.SemaphoreType.DMA(())],
        compiler_params=_cparams(32, 1),
        name="moe_combine",
    )(dest_tiles, y, x1, tg_pad, mod3)


def _layer(x, mod, g_norm1, w_in, pe_cmp_k, pe_cmp_v, w_cmp_k, w_cmp_v, q_gain, k_gain, conv_w, conv_b,
           w_rg, b_rg, w_ig, b_ig, lru_lambda, g_out_nsa, g_out_lru, w_out, g_norm2, w_router, b_router,
           w_e1, b_e1, w_e2, b_e2):
    bsz, seq, d = x.shape
    n = bsz * seq
    xf = x.reshape(n, d)
    mod3 = mod.reshape(bsz, 6, d)

    gate_col = NSA_WIDTH + 6 * KV_WIDTH
    n_gate = 3 * N_HEADS
    w_pad = jnp.concatenate(
        [w_in[:, :gate_col + n_gate], jnp.zeros((d, GATE_PAD - n_gate), w_in.dtype), w_in[:, gate_col + n_gate:]],
        axis=1).astype(bf16)
    zq, zkv, zgl, zx = _inproj(xf, g_norm1, mod3, w_pad, seq)

    qn, ks, vs, kw, vw, gates = _prep(zq, zkv, zgl, q_gain, k_gain, bsz, seq)
    kc, vc = _compress(zkv, pe_cmp_k, pe_cmp_v, w_cmp_k, w_cmp_v, k_gain[0], bsz, seq)
    o_nsa = _attention(qn, kc, vc, ks, vs, kw, vw, gates, bsz, seq)
    o_lru = _lru(zx, conv_w, conv_b, w_rg, b_rg, w_ig, b_ig, lru_lambda, bsz, seq)

    x1, hp, te_pad, tg_pad = _outproj(o_nsa, o_lru, xf, mod3, g_out_nsa, g_out_lru, w_out, g_norm2,
                                      w_router, b_router, seq)

    rank_pad, cnt = _ranks(te_pad)
    counts = cnt[0, :N_EXPERTS].astype(i32)
    pcounts = (counts + MOE_BLK - 1) // MOE_BLK * MOE_BLK
    pends = jnp.cumsum(pcounts)
    pstarts = pends - pcounts
    n_blocks = (n * TOP_K + N_EXPERTS * (MOE_BLK - 1) + MOE_BLK - 1) // MOE_BLK
    n_slots = n_blocks * MOE_BLK
    blk_e = jnp.minimum(
        jnp.searchsorted(pends, jnp.arange(n_blocks, dtype=i32) * MOE_BLK, side="right"), N_EXPERTS - 1).astype(i32)
    n_used = (pends[-1] // MOE_BLK).astype(i32).reshape(1)
    pstart_row = jnp.zeros((1, LANES), i32).at[0, :N_EXPERTS].set(pstarts.astype(i32))
    dest_pad = _slots(te_pad, rank_pad, pstart_row)
    dest = dest_pad[:, :TOP_K]

    xs = _dispatch(dest.reshape(n // TM_DISP, 1, TM_DISP * TOP_K), hp, n_slots)
    act = _expert_up(blk_e, n_used, xs, w_e1, b_e1)
    y = _expert_down(blk_e, n_used, act, w_e2, b_e2)
    out = _combine(dest.reshape(n // TM_COMB, 1, TM_COMB * TOP_K), y, x1, tg_pad, mod3, seq)
    return out.reshape(bsz, seq, d)


def kernel(x, c, w_ada, b_ada, g_norm1, w_in, pe_cmp_k, pe_cmp_v, w_cmp_k, w_cmp_v, q_gain, k_gain, conv_w, conv_b, w_rg, b_rg, w_ig, b_ig, lru_lambda, g_out_nsa, g_out_lru, w_out, g_norm2, w_router, b_router, w_e1, b_e1, w_e2, b_e2):
    for l in range(w_ada.shape[0]):
        mod = _ada_mod(c, w_ada[l], b_ada[l])
        x = _layer(x, mod, g_norm1[l], w_in[l], pe_cmp_k[l], pe_cmp_v[l], w_cmp_k[l], w_cmp_v[l], q_gain[l],
                   k_gain[l], conv_w[l], conv_b[l], w_rg[l], b_rg[l], w_ig[l], b_ig[l], lru_lambda[l],
                   g_out_nsa[l], g_out_lru[l], w_out[l], g_norm2[l], w_router[l], b_router[l], w_e1[l], b_e1[l],
                   w_e2[l], b_e2[l])
    return x
```

```python
import functools

import jax
import jax.numpy as jnp
from jax import lax
from jax.experimental import pallas as pl
from jax.experimental.pallas import tpu as pltpu

f32 = jnp.float32
bf16 = jnp.bfloat16
i32 = jnp.int32

D_MODEL = 2048
N_HEADS = 16
HEAD_DIM = 64
KV_GROUPS = 4
HEADS_PER_GROUP = N_HEADS // KV_GROUPS
NSA_WIDTH = N_HEADS * HEAD_DIM
KV_WIDTH = KV_GROUPS * HEAD_DIM
CMP_LEN = 32
CMP_STRIDE = 16
SEL_LEN = 64
N_SEL = 8
WINDOW = 512
LRU_WIDTH = D_MODEL - NSA_WIDTH
LRU_BLOCKS = 16
CONV_W = 4
RG_C = 8.0
N_EXPERTS = 32
TOP_K = 4
D_FF = D_MODEL
SWIGLU_LIMIT = 7.0
SWIGLU_ALPHA = 1.702
NORM_EPS = 1e-6
NEG_INF = -1e30
SEL_FORCE = 1e30
GATE_PAD = 128
IN_PAD = NSA_WIDTH + 6 * KV_WIDTH + GATE_PAD + 2 * LRU_WIDTH
LANES = 128
MIB = 1024 * 1024

TM_PROJ = 256
TQ = 128
KEY_CHUNK = 512
STRIP = 16
STRIP_UNROLL = 32
TS_LRU = 256
TM_ROUTE = 512
MOE_BLK = 256
FILL_ROWS = MOE_BLK + 8
TM_DISP = 256
TM_COMB = 128
TF_UP = 1024
TN_DOWN = 2048


def _cparams(vmem_mib, n_axes):
    return pltpu.CompilerParams(
        vmem_limit_bytes=int(vmem_mib * MIB),
        dimension_semantics=("arbitrary",) * n_axes,
    )


def _dot(a, b):
    return jnp.dot(a, b, preferred_element_type=f32)


def _dot_nt(a, b):
    return lax.dot_general(a, b, (((1,), (1,)), ((), ())), preferred_element_type=f32)


def _split_bf16(x):
    hi = x.astype(bf16)
    lo = (x - hi.astype(f32)).astype(bf16)
    return hi, lo


def _group_meansq(x, group):
    w = x.shape[1]
    r = lax.broadcasted_iota(i32, (w, w), 0) // group
    c = lax.broadcasted_iota(i32, (w, w), 1) // group
    ones_bd = jnp.where(r == c, 1.0, 0.0).astype(bf16)
    hi, lo = _split_bf16(x * x)
    return (_dot(hi, ones_bd) + _dot(lo, ones_bd)) * (1.0 / group)


def _rms_rows(x, gain):
    ms = jnp.mean(x * x, axis=-1, keepdims=True)
    return x * lax.rsqrt(ms + NORM_EPS) * gain


def _masked_softmax(s, m):
    sm = jnp.where(m, s, NEG_INF)
    mx = jnp.max(sm, axis=-1, keepdims=True)
    e = jnp.where(m, jnp.exp(sm - mx), 0.0)
    den = jnp.sum(e, axis=-1, keepdims=True)
    inv = jnp.where(den > 0.0, 1.0 / den, 0.0)
    return e * inv


def _bias_softmax(sb):
    mx = jnp.max(sb, axis=-1, keepdims=True)
    e = jnp.exp(sb - mx)
    den = jnp.sum(e, axis=-1, keepdims=True)
    return (e * (1.0 / den)).astype(bf16)


def _ada_kernel(c_ref, w_ref, b_ref, o_ref):
    c = c_ref[...]
    sc = c * jax.nn.sigmoid(c)
    o_ref[...] = _dot(sc.astype(bf16), w_ref[...].astype(bf16)) + b_ref[...]


def _ada_mod(c, w_ada, b_ada):
    bsz, d = c.shape
    n = w_ada.shape[1]
    tn = 1024
    return pl.pallas_call(
        _ada_kernel,
        grid=(n // tn,),
        in_specs=[
            pl.BlockSpec((bsz, d), lambda j: (0, 0)),
            pl.BlockSpec((d, tn), lambda j: (0, j)),
            pl.BlockSpec((1, tn), lambda j: (0, j)),
        ],
        out_specs=pl.BlockSpec((bsz, tn), lambda j: (0, j)),
        out_shape=jax.ShapeDtypeStruct((bsz, n), f32),
        compiler_params=_cparams(40, 1),
        name="ada_mod",
    )(c, w_ada, b_ada.reshape(1, n))


def _inproj_kernel(x_ref, g_ref, mod_ref, w_ref, zq_ref, zkv_ref, zgl_ref, zx_ref):
    x = x_ref[...]
    y = _rms_rows(x, g_ref[...])
    h = y * (1.0 + mod_ref[1:2, :]) + mod_ref[0:1, :]
    z = _dot(h.astype(bf16), w_ref[...])
    o = 0
    for ref in (zq_ref, zkv_ref, zgl_ref, zx_ref):
        wdt = ref.shape[1]
        ref[...] = z[:, o:o + wdt]
        o += wdt


def _inproj(xf, g1, mod3, w_pad, seq):
    n, d = xf.shape
    tm = TM_PROJ
    tpb = seq // tm
    widths = (NSA_WIDTH, 6 * KV_WIDTH, GATE_PAD, 2 * LRU_WIDTH)
    return pl.pallas_call(
        _inproj_kernel,
        grid=(n // tm,),
        in_specs=[
            pl.BlockSpec((tm, d), lambda i: (i, 0)),
            pl.BlockSpec((1, d), lambda i: (0, 0)),
            pl.BlockSpec((None, 6, d), lambda i: (i // tpb, 0, 0)),
            pl.BlockSpec((d, IN_PAD), lambda i: (0, 0), pipeline_mode=pl.Buffered(1)),
        ],
        out_specs=[pl.BlockSpec((tm, w), lambda i: (i, 0)) for w in widths],
        out_shape=[jax.ShapeDtypeStruct((n, w), f32) for w in widths],
        compiler_params=_cparams(56, 1),
        name="in_proj",
    )(xf, g1.reshape(1, d), mod3, w_pad)


def _prep_kernel(zq_ref, zkv_ref, zgl_ref, qg_ref, kg_ref, qn_ref, ks_ref, vs_ref, kw_ref, vw_ref, gate_ref):
    gw = HEADS_PER_GROUP * HEAD_DIM
    qg = qg_ref[...]
    for g in range(KV_GROUPS):
        xg = zq_ref[:, g * gw:(g + 1) * gw]
        ms = _group_meansq(xg, HEAD_DIM)
        qn = xg * lax.rsqrt(ms + NORM_EPS) * qg * (HEAD_DIM ** -0.5)
        qn_ref[:, g * gw:(g + 1) * gw] = qn.astype(bf16)

    def norm_k(col, row):
        xk = zkv_ref[:, col * KV_WIDTH:(col + 1) * KV_WIDTH]
        ms = _group_meansq(xk, HEAD_DIM)
        return xk * lax.rsqrt(ms + NORM_EPS) * kg_ref[row:row + 1, :]

    ksn = norm_k(2, 1)
    kwn = norm_k(4, 2)
    vs = zkv_ref[:, 3 * KV_WIDTH:4 * KV_WIDTH]
    vw = zkv_ref[:, 5 * KV_WIDTH:6 * KV_WIDTH]
    gl = jax.nn.sigmoid(zgl_ref[...])
    ng = 3 * HEADS_PER_GROUP
    for g in range(KV_GROUPS):
        sl = slice(g * HEAD_DIM, (g + 1) * HEAD_DIM)
        ks_ref[g] = ksn[:, sl].astype(bf16)
        vs_ref[g] = vs[:, sl].astype(bf16)
        kw_ref[g] = kwn[:, sl].astype(bf16)
        vw_ref[g] = vw[:, sl].astype(bf16)
        gate_ref[g] = gl[:, g * ng:(g + 1) * ng]


def _prep(zq, zkv, zgl, q_gain, k_gain, bsz, seq):
    n = zq.shape[0]
    tm = TM_PROJ
    tpb = seq // tm
    qg = jnp.tile(q_gain.reshape(1, HEAD_DIM), (1, HEADS_PER_GROUP))
    kg = jnp.tile(k_gain.reshape(3, HEAD_DIM), (1, KV_GROUPS))
    hm = lambda w: pl.BlockSpec((None, KV_GROUPS, tm, w), lambda i: (i // tpb, 0, i % tpb, 0))
    hshape = lambda w, dt: jax.ShapeDtypeStruct((bsz, KV_GROUPS, seq, w), dt)
    ng = 3 * HEADS_PER_GROUP
    return pl.pallas_call(
        _prep_kernel,
        grid=(n // tm,),
        in_specs=[
            pl.BlockSpec((tm, NSA_WIDTH), lambda i: (i, 0)),
            pl.BlockSpec((tm, 6 * KV_WIDTH), lambda i: (i, 0)),
            pl.BlockSpec((tm, GATE_PAD), lambda i: (i, 0)),
            pl.BlockSpec((1, HEADS_PER_GROUP * HEAD_DIM), lambda i: (0, 0)),
            pl.BlockSpec((3, KV_WIDTH), lambda i: (0, 0)),
        ],
        out_specs=[
            pl.BlockSpec((tm, NSA_WIDTH), lambda i: (i, 0)),
            hm(HEAD_DIM), hm(HEAD_DIM), hm(HEAD_DIM), hm(HEAD_DIM), hm(ng),
        ],
        out_shape=[
            jax.ShapeDtypeStruct((n, NSA_WIDTH), bf16),
            hshape(HEAD_DIM, bf16), hshape(HEAD_DIM, bf16), hshape(HEAD_DIM, bf16), hshape(HEAD_DIM, bf16),
            hshape(ng, f32),
        ],
        compiler_params=_cparams(32, 1),
        name="nsa_prep",
    )(zq, zkv, zgl, qg, kg)


def _compress_kernel(fk_ref, fv_ref, pek_ref, pev_ref, wk_ref, wv_ref, kg_ref, kc_ref, vc_ref):
    kc = _dot((fk_ref[...] + pek_ref[...]).astype(bf16), wk_ref[...])
    vc = _dot((fv_ref[...] + pev_ref[...]).astype(bf16), wv_ref[...])
    kc_ref[...] = _rms_rows(kc, kg_ref[...]).astype(bf16)
    vc_ref[...] = vc.astype(bf16)


def _flat_blocks(z, bsz, seq):
    nsub = seq // CMP_STRIDE
    sub = z.reshape(bsz, nsub, CMP_STRIDE, KV_GROUPS, HEAD_DIM)
    nxt = jnp.concatenate([sub[:, 1:], jnp.zeros_like(sub[:, :1])], axis=1)
    blocks = jnp.concatenate([sub, nxt], axis=2)
    return blocks.transpose(0, 3, 1, 2, 4).reshape(bsz, KV_GROUPS, nsub, CMP_LEN * HEAD_DIM)


def _compress(zkv, pe_k, pe_v, w_ck, w_cv, k_gain0, bsz, seq):
    nsub = seq // CMP_STRIDE
    fk = _flat_blocks(zkv[:, 0:KV_WIDTH], bsz, seq)
    fv = _flat_blocks(zkv[:, KV_WIDTH:2 * KV_WIDTH], bsz, seq)
    kdim = CMP_LEN * HEAD_DIM
    blk = pl.BlockSpec((None, None, nsub, kdim), lambda b, g: (b, g, 0, 0))
    cst = lambda r, c: pl.BlockSpec((r, c), lambda b, g: (0, 0))
    oblk = pl.BlockSpec((None, None, nsub, HEAD_DIM), lambda b, g: (b, g, 0, 0))
    oshape = jax.ShapeDtypeStruct((bsz, KV_GROUPS, nsub, HEAD_DIM), bf16)
    return pl.pallas_call(
        _compress_kernel,
        grid=(bsz, KV_GROUPS),
        in_specs=[blk, blk, cst(1, kdim), cst(1, kdim), cst(kdim, HEAD_DIM), cst(kdim, HEAD_DIM), cst(1, HEAD_DIM)],
        out_specs=[oblk, oblk],
        out_shape=[oshape, oshape],
        compiler_params=_cparams(32, 2),
        name="nsa_compress",
    )(fk, fv, pe_k.reshape(1, kdim), pe_v.reshape(1, kdim), w_ck.astype(bf16), w_cv.astype(bf16),
      k_gain0.reshape(1, HEAD_DIM))


def _softmax_strips(s_ref, bias_ref, p_ref, r_ref, width):
    n_strips = s_ref.shape[0] // STRIP
    wide = lambda v: jnp.concatenate([v] * (width // LANES), axis=1)
    lanes = lambda v: jnp.broadcast_to(v, (STRIP, LANES))

    def rows(i):
        return pl.ds(pl.multiple_of(i * STRIP, STRIP), STRIP)

    def row_max(i, c):
        tk = pl.ds(pl.multiple_of(lax.rem(i * STRIP, TQ), STRIP), STRIP)
        sb = s_ref[rows(i), 0:width] + bias_ref[tk, 0:width]
        s_ref[rows(i), 0:width] = sb
        r_ref[rows(i), :] = lanes(jnp.max(sb, axis=-1, keepdims=True))
        return c

    def exp_sum(i, c):
        e = jnp.exp(s_ref[rows(i), 0:width] - wide(r_ref[rows(i), :]))
        s_ref[rows(i), 0:width] = e
        r_ref[rows(i), :] = lanes(1.0 / jnp.sum(e, axis=-1, keepdims=True))
        return c

    def scale(i, c):
        p_ref[rows(i), 0:width] = (s_ref[rows(i), 0:width] * wide(r_ref[rows(i), :])).astype(bf16)
        return c

    for phase in (row_max, exp_sum, scale):
        lax.fori_loop(0, n_strips, phase, 0, unroll=STRIP_UNROLL)


def _attn_kernel(q_ref, kc_ref, vc_ref, ks_ref, vs_ref, kw_ref, vw_ref, gate_ref, o_ref,
                 os_ref, s_ref, p_ref, bias_ref, r_ref, *, seq):
    t = pl.program_id(2)
    hg = HEADS_PER_GROUP
    q = q_ref[...]
    qs = jnp.concatenate([q[:, h * HEAD_DIM:(h + 1) * HEAD_DIM] for h in range(hg)], axis=0)
    tpos = t * TQ + lax.broadcasted_iota(i32, (TQ, 1), 0)
    tpos4 = jnp.concatenate([tpos] * hg, axis=0)

    ncmp = kc_ref.shape[0]
    nblk = seq // SEL_LEN
    s_c = _dot_nt(qs, kc_ref[...])
    cstart = lax.broadcasted_iota(i32, (1, ncmp), 1) * CMP_STRIDE
    p_c = _masked_softmax(s_c, (cstart + (CMP_LEN - 1)) <= tpos4).astype(bf16)
    o_c = _dot(p_c, vc_ref[...])

    cs = lax.broadcasted_iota(i32, (nblk, ncmp), 1) * CMP_STRIDE
    ss = lax.broadcasted_iota(i32, (nblk, ncmp), 0) * SEL_LEN
    overlap_t = jnp.where((cs < ss + SEL_LEN) & (cs + CMP_LEN > ss), 1.0, 0.0).astype(bf16)
    imp4 = _dot_nt(overlap_t, p_c)
    imp = imp4[:, 0:TQ]
    for h in range(1, hg):
        imp = imp + imp4[:, h * TQ:(h + 1) * TQ]
    j = lax.broadcasted_iota(i32, (nblk, TQ), 0)
    qblk = jnp.right_shift(t * TQ + lax.broadcasted_iota(i32, (nblk, TQ), 1), 6)
    forced = (j == 0) | (j == qblk) | (j == qblk - 1)
    impf = jnp.where(forced, SEL_FORCE, jnp.where(j <= qblk, imp, -SEL_FORCE))
    beaten = jnp.zeros((nblk, TQ), i32)
    for i in range(nblk):
        ci = impf[i:i + 1, :]
        beats = (ci > impf) | ((ci == impf) & (j > i))
        beaten = beaten + jnp.where(beats, 1, 0)
    sel_bias = jnp.where(beaten < min(N_SEL, nblk), 0.0, NEG_INF).T.astype(bf16)

    n_chunks = (t * TQ + TQ + KEY_CHUNK - 1) // KEY_CHUNK
    for c in range(seq // KEY_CHUNK):

        @pl.when(n_chunks == c + 1)
        def _(width=(c + 1) * KEY_CHUNK):
            nb = width // SEL_LEN
            eb = lax.broadcasted_iota(i32, (nb, width), 0)
            ep = jnp.right_shift(lax.broadcasted_iota(i32, (nb, width), 1), 6)
            key_bias = _dot(sel_bias[:, :nb], jnp.where(eb == ep, 1.0, 0.0).astype(bf16))
            kpos = lax.broadcasted_iota(i32, (1, width), 1)
            bias_ref[:, 0:width] = jnp.where(kpos <= tpos, key_bias, NEG_INF)
            s_ref[:, 0:width] = _dot_nt(qs, ks_ref[0:width, :])
            _softmax_strips(s_ref, bias_ref, p_ref, r_ref, width)
            os_ref[...] = _dot(p_ref[:, 0:width], vs_ref[0:width, :])

    o_s = os_ref[...]

    wk = WINDOW + TQ
    start = pl.multiple_of(jnp.maximum(t * TQ - WINDOW, 0), TQ)
    wpos = start + lax.broadcasted_iota(i32, (1, wk), 1)
    bias_ref[:, 0:wk] = jnp.where((wpos <= tpos) & (wpos > tpos - WINDOW), 0.0, NEG_INF)
    s_ref[:, 0:wk] = _dot_nt(qs, kw_ref[pl.ds(start, wk), :])
    _softmax_strips(s_ref, bias_ref, p_ref, r_ref, wk)
    o_w = _dot(p_ref[:, 0:wk], vw_ref[pl.ds(start, wk), :])

    gts = gate_ref[...]
    outs = []
    for h in range(hg):
        rows = slice(h * TQ, (h + 1) * TQ)
        outs.append(gts[:, 3 * h:3 * h + 1] * o_c[rows] + gts[:, 3 * h + 1:3 * h + 2] * o_s[rows]
                    + gts[:, 3 * h + 2:3 * h + 3] * o_w[rows])
    o_ref[...] = jnp.concatenate(outs, axis=1)


def _attention(qn, kc, vc, ks, vs, kw, vw, gates, bsz, seq):
    n = qn.shape[0]
    gw = HEADS_PER_GROUP * HEAD_DIM
    ntq = seq // TQ
    nsub = seq // CMP_STRIDE
    ng = 3 * HEADS_PER_GROUP
    qblk = pl.BlockSpec((TQ, gw), lambda b, g, t: (b * ntq + t, g))
    full = lambda r, w: pl.BlockSpec((None, None, r, w), lambda b, g, t: (b, g, 0, 0))
    return pl.pallas_call(
        functools.partial(_attn_kernel, seq=seq),
        grid=(bsz, KV_GROUPS, ntq),
        in_specs=[
            qblk, full(nsub, HEAD_DIM), full(nsub, HEAD_DIM),
            full(seq, HEAD_DIM), full(seq, HEAD_DIM), full(seq, HEAD_DIM), full(seq, HEAD_DIM),
            pl.BlockSpec((None, None, TQ, ng), lambda b, g, t: (b, g, t, 0)),
        ],
        out_specs=qblk,
        out_shape=jax.ShapeDtypeStruct((n, NSA_WIDTH), f32),
        scratch_shapes=[
            pltpu.VMEM((HEADS_PER_GROUP * TQ, HEAD_DIM), f32),
            pltpu.VMEM((HEADS_PER_GROUP * TQ, seq), f32),
            pltpu.VMEM((HEADS_PER_GROUP * TQ, seq), bf16),
            pltpu.VMEM((TQ, seq), f32),
            pltpu.VMEM((HEADS_PER_GROUP * TQ, LANES), f32),
        ],
        compiler_params=_cparams(48, 3),
        name="nsa_attention",
    )(qn, kc, vc, ks, vs, kw, vw, gates)


def _lru_kernel(xr_ref, xg_ref, cw_ref, cb_ref, wr_ref, br_ref, wi_ref, bi_ref, lam_ref, o_ref,
                xbuf, hstate, abuf, ubuf):
    ts = xr_ref.shape[0]
    s = pl.program_id(1)

    @pl.when(s == 0)
    def _():
        xbuf[0:8, :] = jnp.zeros((8, LRU_WIDTH), f32)
        hstate[...] = jnp.zeros_like(hstate)

    xbuf[8:8 + ts, :] = xr_ref[...]
    acc = xbuf[5:5 + ts, :] * cw_ref[0:1, :]
    for k in range(1, CONV_W):
        acc = acc + xbuf[5 + k:5 + k + ts, :] * cw_ref[k:k + 1, :]
    xc = cb_ref[...] + acc
    xbuf[0:8, :] = xbuf[ts:ts + 8, :]

    xcb = xc.astype(bf16)
    r = jax.nn.sigmoid(_dot(xcb, wr_ref[...]) + br_ref[...])
    ig = jax.nn.sigmoid(_dot(xcb, wi_ref[...]) + bi_ref[...])
    nl = -lam_ref[...]
    softplus = jnp.maximum(nl, 0.0) + jnp.log1p(jnp.exp(-jnp.abs(nl)))
    log_a = (-RG_C * softplus) * r
    abuf[...] = jnp.exp(log_a)
    th = jnp.tanh(log_a)
    ubuf[...] = jnp.sqrt(-2.0 * th / (1.0 - th)) * (ig * xc)

    def step(jb, h):
        base = pl.multiple_of(jb * 8, 8)
        a8 = abuf[pl.ds(base, 8), :]
        u8 = ubuf[pl.ds(base, 8), :]
        rows = []
        for rr in range(8):
            h = a8[rr:rr + 1, :] * h + u8[rr:rr + 1, :]
            rows.append(h)
        ubuf[pl.ds(base, 8), :] = jnp.concatenate(rows, axis=0)
        return h

    hstate[0:1, :] = lax.fori_loop(0, ts // 8, step, hstate[0:1, :])
    xg = xg_ref[...]
    gelu = 0.5 * xg * (1.0 + jnp.tanh(0.7978845608028654 * (xg + 0.044715 * (xg * xg * xg))))
    o_ref[...] = ubuf[...] * gelu


def _block_diag(w):
    nb, bw, _ = w.shape
    eye = jnp.eye(nb, dtype=w.dtype)
    return (w[:, :, None, :] * eye[:, None, :, None]).reshape(nb * bw, nb * bw)


def _lru(zx, conv_w, conv_b, w_rg, b_rg, w_ig, b_ig, lam, bsz, seq):
    n = zx.shape[0]
    ts = TS_LRU
    tpb = seq // ts
    w = LRU_WIDTH
    row = lambda v: v.reshape(1, w)
    cst = lambda r, c: pl.BlockSpec((r, c), lambda b, s: (0, 0))
    return pl.pallas_call(
        _lru_kernel,
        grid=(bsz, tpb),
        in_specs=[
            pl.BlockSpec((ts, w), lambda b, s: (b * tpb + s, 0)),
            pl.BlockSpec((ts, w), lambda b, s: (b * tpb + s, 1)),
            cst(CONV_W, w), cst(1, w), cst(w, w), cst(1, w), cst(w, w), cst(1, w), cst(1, w),
        ],
        out_specs=pl.BlockSpec((ts, w), lambda b, s: (b * tpb + s, 0)),
        out_shape=jax.ShapeDtypeStruct((n, w), f32),
        scratch_shapes=[
            pltpu.VMEM((ts + 8, w), f32), pltpu.VMEM((8, w), f32),
            pltpu.VMEM((ts, w), f32), pltpu.VMEM((ts, w), f32),
        ],
        compiler_params=_cparams(40, 2),
        name="rglru",
    )(zx, zx, conv_w, row(conv_b), _block_diag(w_rg).astype(bf16), row(b_rg),
      _block_diag(w_ig).astype(bf16), row(b_ig), row(lam))


def _outproj_kernel(on_ref, ol_ref, x_ref, mod_ref, gn_ref, gl_ref, w_ref, g2_ref, wrh_ref, wrl_ref, br_ref,
                    x1_ref, hp_ref, te_ref, tg_ref):
    nn = _rms_rows(on_ref[...], gn_ref[...])
    nl = _rms_rows(ol_ref[...], gl_ref[...])
    hcat = jnp.concatenate([nn, nl], axis=1).astype(bf16)
    mix = _dot(hcat, w_ref[...])
    x1 = x_ref[...] + mod_ref[2:3, :] * mix
    x1_ref[...] = x1
    h2 = _rms_rows(x1, g2_ref[...]) * (1.0 + mod_ref[4:5, :]) + mod_ref[3:4, :]

    half = D_MODEL // 2
    hb = h2.astype(bf16).astype(f32)
    hi_bits = lax.bitcast_convert_type(hb[:, :half], i32)
    lo_bits = lax.shift_right_logical(lax.bitcast_convert_type(hb[:, half:], i32), 16)
    hp_ref[...] = hi_bits | lo_bits

    hh, hl = _split_bf16(h2)
    logits = _dot(hh, wrh_ref[...]) + _dot(hl, wrh_ref[...]) + _dot(hh, wrl_ref[...]) + br_ref[...]
    lane = lax.broadcasted_iota(i32, (1, LANES), 1)
    lane_f = lane.astype(f32)
    vals, idxs = [], []
    cur = logits
    for _ in range(TOP_K):
        m = jnp.max(cur, axis=-1, keepdims=True)
        idx = jnp.min(jnp.where(cur == m, lane_f, float(LANES)), axis=-1, keepdims=True).astype(i32)
        vals.append(m)
        idxs.append(idx)
        cur = jnp.where(lane == idx, -3e38, cur)
    es = [jnp.exp(v - vals[0]) for v in vals]
    den = es[0]
    for e in es[1:]:
        den = den + e
    inv = 1.0 / den
    te = jnp.full(logits.shape, -1, i32)
    tg = jnp.zeros(logits.shape, f32)
    for k in range(TOP_K):
        te = jnp.where(lane == k, idxs[k], te)
        tg = jnp.where(lane == k, es[k] * inv, tg)
    te_ref[...] = te
    tg_ref[...] = tg


def _outproj(o_nsa, o_lru, xf, mod3, g_out_nsa, g_out_lru, w_out, g_norm2, w_router, b_router, seq):
    n, d = xf.shape
    tm = TM_PROJ
    tpb = seq // tm
    wr = jnp.zeros((d, LANES), f32).at[:, :N_EXPERTS].set(w_router)
    wrh = wr.astype(bf16)
    wrl = (wr - wrh.astype(f32)).astype(bf16)
    br = jnp.full((1, LANES), NEG_INF, f32).at[0, :N_EXPERTS].set(b_router)
    rows = lambda w: pl.BlockSpec((tm, w), lambda i: (i, 0))
    cst = lambda r, c: pl.BlockSpec((r, c), lambda i: (0, 0))
    return pl.pallas_call(
        _outproj_kernel,
        grid=(n // tm,),
        in_specs=[
            rows(NSA_WIDTH), rows(LRU_WIDTH), rows(d),
            pl.BlockSpec((None, 6, d), lambda i: (i // tpb, 0, 0)),
            cst(1, NSA_WIDTH), cst(1, LRU_WIDTH), cst(d, d), cst(1, d), cst(d, LANES), cst(d, LANES), cst(1, LANES),
        ],
        out_specs=[rows(d), rows(d // 2), rows(LANES), rows(LANES)],
        out_shape=[
            jax.ShapeDtypeStruct((n, d), f32), jax.ShapeDtypeStruct((n, d // 2), i32),
            jax.ShapeDtypeStruct((n, LANES), i32), jax.ShapeDtypeStruct((n, LANES), f32),
        ],
        compiler_params=_cparams(48, 1),
        name="out_proj_router",
    )(o_nsa, o_lru, xf, mod3, g_out_nsa.reshape(1, -1), g_out_lru.reshape(1, -1), w_out.astype(bf16),
      g_norm2.reshape(1, d), wrh, wrl, br)


def _rank_kernel(te_ref, rank_ref, cnt_ref, carry):
    i = pl.program_id(0)
    tm = te_ref.shape[0]

    @pl.when(i == 0)
    def _():
        carry[...] = jnp.zeros_like(carry)

    te = te_ref[...]
    lane = lax.broadcasted_iota(i32, (1, LANES), 1)
    hits = [te[:, k:k + 1] == lane for k in range(TOP_K)]
    onehot = jnp.zeros((tm, LANES), f32)
    for h in hits:
        onehot = onehot + jnp.where(h, 1.0, 0.0)
    r = lax.broadcasted_iota(i32, (tm, tm), 0)
    c = lax.broadcasted_iota(i32, (tm, tm), 1)
    lower = jnp.where(c < r, 1.0, 0.0).astype(bf16)
    prefix = _dot(lower, onehot.astype(bf16)) + carry[0:1, :]
    rank = jnp.zeros((tm, LANES), f32)
    for k in range(TOP_K):
        rk = jnp.sum(jnp.where(hits[k], prefix, 0.0), axis=-1, keepdims=True)
        rank = jnp.where(lane == k, rk, rank)
    rank_ref[...] = rank.astype(i32)
    carry[0:1, :] = carry[0:1, :] + jnp.sum(onehot, axis=0, keepdims=True)
    cnt_ref[...] = carry[...]


def _ranks(te_pad):
    n = te_pad.shape[0]
    tm = TM_ROUTE
    return pl.pallas_call(
        _rank_kernel,
        grid=(n // tm,),
        in_specs=[pl.BlockSpec((tm, LANES), lambda i: (i, 0))],
        out_specs=[pl.BlockSpec((tm, LANES), lambda i: (i, 0)), pl.BlockSpec((8, LANES), lambda i: (0, 0))],
        out_shape=[jax.ShapeDtypeStruct((n, LANES), i32), jax.ShapeDtypeStruct((8, LANES), f32)],
        scratch_shapes=[pltpu.VMEM((8, LANES), f32)],
        compiler_params=_cparams(32, 1),
        name="route_ranks",
    )(te_pad)


def _slot_kernel(te_ref, rank_ref, pstart_ref, dest_ref):
    te = te_ref[...]
    lane = lax.broadcasted_iota(i32, (1, LANES), 1)
    ps = pstart_ref[...].astype(f32)
    dest = jnp.zeros(te.shape, i32)
    for k in range(TOP_K):
        base = jnp.sum(jnp.where(te[:, k:k + 1] == lane, ps, 0.0), axis=-1, keepdims=True)
        dest = jnp.where(lane == k, base.astype(i32), dest)
    dest_ref[...] = dest + rank_ref[...]


def _slots(te_pad, rank_pad, pstart_row):
    n = te_pad.shape[0]
    tm = TM_ROUTE
    blk = pl.BlockSpec((tm, LANES), lambda i: (i, 0))
    return pl.pallas_call(
        _slot_kernel,
        grid=(n // tm,),
        in_specs=[blk, blk, pl.BlockSpec((1, LANES), lambda i: (0, 0))],
        out_specs=blk,
        out_shape=jax.ShapeDtypeStruct((n, LANES), i32),
        compiler_params=_cparams(32, 1),
        name="route_slots",
    )(te_pad, rank_pad, pstart_row)


def _row_copy(src, src_row, dst, dst_row, sem):
    return pltpu.make_async_copy(src.at[pl.ds(src_row, 1)], dst.at[pl.ds(dst_row, 1)], sem)


def _dispatch_kernel(dest_ref, fill_ref, h_ref, xs_ref, zbuf, sem, zsem):
    tm = h_ref.shape[0]

    @pl.when(pl.program_id(0) == 0)
    def _():
        zbuf[...] = jnp.zeros_like(zbuf)

        def fill(e, c):
            start = pl.multiple_of(fill_ref[e], 8)
            cp = pltpu.make_async_copy(zbuf, xs_ref.at[pl.ds(start, FILL_ROWS)], zsem)
            cp.start()
            cp.wait()
            return c

        lax.fori_loop(0, fill_ref.shape[0], fill, 0)

    def issue(i, c):
        for k in range(TOP_K):
            _row_copy(h_ref, i, xs_ref, dest_ref[0, i * TOP_K + k], sem).start(priority=k % 2)
        return c

    lax.fori_loop(0, tm, issue, 0)

    def drain(i, c):
        for k in range(TOP_K):
            _row_copy(h_ref, 0, xs_ref, 0, sem).wait()
        return c

    lax.fori_loop(0, tm, drain, 0)


def _dispatch(dest_tiles, fill_start, hp, n_slots):
    n, w = hp.shape
    tm = TM_DISP
    return pl.pallas_call(
        _dispatch_kernel,
        grid=(n // tm,),
        in_specs=[
            pl.BlockSpec((None, 1, tm * TOP_K), lambda i: (i, 0, 0), memory_space=pltpu.SMEM),
            pl.BlockSpec(memory_space=pltpu.SMEM),
            pl.BlockSpec((tm, w), lambda i: (i, 0)),
        ],
        out_specs=pl.BlockSpec(memory_space=pl.ANY),
        out_shape=jax.ShapeDtypeStruct((n_slots, w), i32),
        scratch_shapes=[pltpu.VMEM((FILL_ROWS, w), i32), pltpu.SemaphoreType.DMA(()), pltpu.SemaphoreType.DMA(())],
        compiler_params=_cparams(32, 1),
        name="moe_dispatch",
    )(dest_tiles, fill_start, hp)


def _unpack_rows(xp):
    hi = lax.bitcast_convert_type(xp & jnp.int32(-65536), f32)
    lo = lax.bitcast_convert_type(lax.shift_left(xp, 16), f32)
    return jnp.concatenate([hi, lo], axis=1).astype(bf16)


def _new_expert(be_ref, i):
    return (i == 0) | (be_ref[i] != be_ref[jnp.maximum(i - 1, 0)])


def _up_kernel(be_ref, nb_ref, xs_ref, wg_ref, wl_ref, bg_ref, bl_ref, act_ref, wgb, wlb):
    i = pl.program_id(1)
    live = i < nb_ref[0]

    @pl.when(live & _new_expert(be_ref, i))
    def _():
        wgb[...] = wg_ref[...].astype(bf16)
        wlb[...] = wl_ref[...].astype(bf16)

    @pl.when(live)
    def _():
        x = _unpack_rows(xs_ref[...])
        ug = _dot(x, wgb[...]) + bg_ref[...]
        ul = _dot(x, wlb[...]) + bl_ref[...]
        ug = jnp.minimum(ug, SWIGLU_LIMIT)
        ul = jnp.clip(ul, -SWIGLU_LIMIT, SWIGLU_LIMIT)
        act_ref[...] = (ug * jax.nn.sigmoid(SWIGLU_ALPHA * ug) * (ul + 1.0)).astype(bf16)

    @pl.when(pl.program_id(1) >= nb_ref[0])
    def _():
        act_ref[...] = jnp.zeros_like(act_ref)


def _expert_up(blk_e, n_used, xs, w_e1, b_e1):
    n_slots, w = xs.shape
    nblk = n_slots // MOE_BLK
    tf = TF_UP
    ncol = D_FF // tf
    d = D_MODEL
    grid_spec = pltpu.PrefetchScalarGridSpec(
        num_scalar_prefetch=2,
        grid=(ncol, nblk),
        in_specs=[
            pl.BlockSpec((MOE_BLK, w), lambda n, i, be, nb: (jnp.minimum(i, nb[0] - 1), 0)),
            pl.BlockSpec((None, d, tf), lambda n, i, be, nb: (be[i], 0, n)),
            pl.BlockSpec((None, d, tf), lambda n, i, be, nb: (be[i], 0, ncol + n)),
            pl.BlockSpec((None, 1, tf), lambda n, i, be, nb: (be[i], 0, n)),
            pl.BlockSpec((None, 1, tf), lambda n, i, be, nb: (be[i], 0, ncol + n)),
        ],
        out_specs=pl.BlockSpec((MOE_BLK, tf), lambda n, i, be, nb: (i, n)),
        scratch_shapes=[pltpu.VMEM((d, tf), bf16), pltpu.VMEM((d, tf), bf16)],
    )
    return pl.pallas_call(
        _up_kernel,
        grid_spec=grid_spec,
        out_shape=jax.ShapeDtypeStruct((n_slots, D_FF), bf16),
        compiler_params=_cparams(56, 2),
        name="moe_up",
    )(blk_e, n_used, xs, w_e1, w_e1, b_e1.reshape(N_EXPERTS, 1, 2 * D_FF), b_e1.reshape(N_EXPERTS, 1, 2 * D_FF))


def _down_kernel(be_ref, nb_ref, act_ref, w_ref, b_ref, y_ref, wb):
    i = pl.program_id(1)
    live = i < nb_ref[0]

    @pl.when(live & _new_expert(be_ref, i))
    def _():
        wb[...] = w_ref[...].astype(bf16)

    @pl.when(live)
    def _():
        y_ref[...] = _dot(act_ref[...], wb[...]) + b_ref[...]

    @pl.when(pl.program_id(1) >= nb_ref[0])
    def _():
        y_ref[...] = jnp.zeros_like(y_ref)


def _expert_down(blk_e, n_used, act, w_e2, b_e2):
    n_slots = act.shape[0]
    nblk = n_slots // MOE_BLK
    tn = TN_DOWN
    ncol = D_MODEL // tn
    grid_spec = pltpu.PrefetchScalarGridSpec(
        num_scalar_prefetch=2,
        grid=(ncol, nblk),
        in_specs=[
            pl.BlockSpec((MOE_BLK, D_FF), lambda n, i, be, nb: (i, 0)),
            pl.BlockSpec((None, D_FF, tn), lambda n, i, be, nb: (be[i], 0, n)),
            pl.BlockSpec((None, 1, tn), lambda n, i, be, nb: (be[i], 0, n)),
        ],
        out_specs=pl.BlockSpec((MOE_BLK, tn), lambda n, i, be, nb: (i, n)),
        scratch_shapes=[pltpu.VMEM((D_FF, tn), bf16)],
    )
    return pl.pallas_call(
        _down_kernel,
        grid_spec=grid_spec,
        out_shape=jax.ShapeDtypeStruct((n_slots, D_MODEL), f32),
        compiler_params=_cparams(56, 2),
        name="moe_down",
    )(blk_e, n_used, act, w_e2, b_e2.reshape(N_EXPERTS, 1, D_MODEL))


def _combine_kernel(dest_ref, y_ref, x1_ref, tg_ref, mod_ref, o_ref, buf, sem):
    tm = x1_ref.shape[0]

    def issue(i, c):
        for k in range(TOP_K):
            _row_copy(y_ref, dest_ref[0, i * TOP_K + k], buf.at[k], i, sem).start(priority=k % 2)
        return c

    lax.fori_loop(0, tm, issue, 0)

    def drain(i, c):
        for k in range(TOP_K):
            _row_copy(y_ref, 0, buf.at[k], 0, sem).wait()
        return c

    lax.fori_loop(0, tm, drain, 0)
    tg = tg_ref[...]
    acc = tg[:, 0:1] * buf[0]
    for k in range(1, TOP_K):
        acc = acc + tg[:, k:k + 1] * buf[k]
    o_ref[...] = x1_ref[...] + mod_ref[5:6, :] * acc


def _combine(dest_tiles, y, x1, tg_pad, mod3, seq):
    n, d = x1.shape
    tm = TM_COMB
    tpb = seq // tm
    return pl.pallas_call(
        _combine_kernel,
        grid=(n // tm,),
        in_specs=[
            pl.BlockSpec((None, 1, tm * TOP_K), lambda i: (i, 0, 0), memory_space=pltpu.SMEM),
            pl.BlockSpec(memory_space=pl.ANY),
            pl.BlockSpec((tm, d), lambda i: (i, 0)),
            pl.BlockSpec((tm, LANES), lambda i: (i, 0)),
            pl.BlockSpec((None, 6, d), lambda i: (i // tpb, 0, 0)),
        ],
        out_specs=pl.BlockSpec((tm, d), lambda i: (i, 0)),
        out_shape=jax.ShapeDtypeStruct((n, d), f32),
        scratch_shapes=[pltpu.VMEM((TOP_K, tm, d), f32), pltpu.SemaphoreType.DMA(())],
        compiler_params=_cparams(32, 1),
        name="moe_combine",
    )(dest_tiles, y, x1, tg_pad, mod3)


def _layer(x, mod, g_norm1, w_in, pe_cmp_k, pe_cmp_v, w_cmp_k, w_cmp_v, q_gain, k_gain, conv_w, conv_b,
           w_rg, b_rg, w_ig, b_ig, lru_lambda, g_out_nsa, g_out_lru, w_out, g_norm2, w_router, b_router,
           w_e1, b_e1, w_e2, b_e2):
    bsz, seq, d = x.shape
    n = bsz * seq
    xf = x.reshape(n, d)
    mod3 = mod.reshape(bsz, 6, d)

    gate_col = NSA_WIDTH + 6 * KV_WIDTH
    n_gate = 3 * N_HEADS
    w_pad = jnp.concatenate(
        [w_in[:, :gate_col + n_gate], jnp.zeros((d, GATE_PAD - n_gate), w_in.dtype), w_in[:, gate_col + n_gate:]],
        axis=1).astype(bf16)
    zq, zkv, zgl, zx = _inproj(xf, g_norm1, mod3, w_pad, seq)

    qn, ks, vs, kw, vw, gates = _prep(zq, zkv, zgl, q_gain, k_gain, bsz, seq)
    kc, vc = _compress(zkv, pe_cmp_k, pe_cmp_v, w_cmp_k, w_cmp_v, k_gain[0], bsz, seq)
    o_nsa = _attention(qn, kc, vc, ks, vs, kw, vw, gates, bsz, seq)
    o_lru = _lru(zx, conv_w, conv_b, w_rg, b_rg, w_ig, b_ig, lru_lambda, bsz, seq)

    x1, hp, te_pad, tg_pad = _outproj(o_nsa, o_lru, xf, mod3, g_out_nsa, g_out_lru, w_out, g_norm2,
                                      w_router, b_router, seq)

    rank_pad, cnt = _ranks(te_pad)
    counts = cnt[0, :N_EXPERTS].astype(i32)
    pcounts = (counts + MOE_BLK - 1) // MOE_BLK * MOE_BLK
    pends = jnp.cumsum(pcounts)
    pstarts = pends - pcounts
    n_blocks = (n * TOP_K + N_EXPERTS * (MOE_BLK - 1) + MOE_BLK - 1) // MOE_BLK
    n_slots = n_blocks * MOE_BLK
    blk_start = jnp.arange(n_blocks, dtype=i32) * MOE_BLK
    blk_e = jnp.minimum(jnp.sum((pends[None, :] <= blk_start[:, None]).astype(i32), axis=1), N_EXPERTS - 1)
    n_used = (pends[-1] // MOE_BLK).astype(i32).reshape(1)
    n_tail = (n_slots - n * TOP_K + MOE_BLK - 1) // MOE_BLK
    fill_rows = jnp.concatenate([pstarts + counts, pends[-1] + jnp.arange(n_tail, dtype=i32) * MOE_BLK])
    fill_start = jnp.minimum(fill_rows // 8 * 8, n_slots - FILL_ROWS).astype(i32)
    pstart_row = jnp.zeros((1, LANES), i32).at[0, :N_EXPERTS].set(pstarts.astype(i32))
    dest_pad = _slots(te_pad, rank_pad, pstart_row)
    dest = dest_pad[:, :TOP_K]

    xs = _dispatch(dest.reshape(n // TM_DISP, 1, TM_DISP * TOP_K), fill_start, hp, n_slots)
    act = _expert_up(blk_e, n_used, xs, w_e1, b_e1)
    y = _expert_down(blk_e, n_used, act, w_e2, b_e2)
    out = _combine(dest.reshape(n // TM_COMB, 1, TM_COMB * TOP_K), y, x1, tg_pad, mod3, seq)
    return out.reshape(bsz, seq, d)


def kernel(x, c, w_ada, b_ada, g_norm1, w_in, pe_cmp_k, pe_cmp_v, w_cmp_k, w_cmp_v, q_gain, k_gain, conv_w, conv_b, w_rg, b_rg, w_ig, b_ig, lru_lambda, g_out_nsa, g_out_lru, w_out, g_norm2, w_router, b_router, w_e1, b_e1, w_e2, b_e2):
    for l in range(w_ada.shape[0]):
        mod = _ada_mod(c, w_ada[l], b_ada[l])
        x = _layer(x, mod, g_norm1[l], w_in[l], pe_cmp_k[l], pe_cmp_v[l], w_cmp_k[l], w_cmp_v[l], q_gain[l],
                   k_gain[l], conv_w[l], conv_b[l], w_rg[l], b_rg[l], w_ig[l], b_ig[l], lru_lambda[l],
                   g_out_nsa[l], g_out_lru[l], w_out[l], g_norm2[l], w_router[l], b_router[l], w_e1[l], b_e1[l],
                   w_e2[l], b_e2[l])
    return x
```

```python
import functools

import jax
import jax.numpy as jnp
from jax import lax
from jax.experimental import pallas as pl
from jax.experimental.pallas import tpu as pltpu

f32 = jnp.float32
bf16 = jnp.bfloat16
i32 = jnp.int32

D_MODEL = 2048
N_HEADS = 16
HEAD_DIM = 64
KV_GROUPS = 4
HEADS_PER_GROUP = N_HEADS // KV_GROUPS
NSA_WIDTH = N_HEADS * HEAD_DIM
KV_WIDTH = KV_GROUPS * HEAD_DIM
CMP_LEN = 32
CMP_STRIDE = 16
SEL_LEN = 64
N_SEL = 8
WINDOW = 512
LRU_WIDTH = D_MODEL - NSA_WIDTH
LRU_BLOCKS = 16
CONV_W = 4
RG_C = 8.0
N_EXPERTS = 32
TOP_K = 4
D_FF = D_MODEL
SWIGLU_LIMIT = 7.0
SWIGLU_ALPHA = 1.702
NORM_EPS = 1e-6
NEG_INF = -1e30
SEL_FORCE = 1e30
GATE_PAD = 128
IN_PAD = NSA_WIDTH + 6 * KV_WIDTH + GATE_PAD + 2 * LRU_WIDTH
LANES = 128
MIB = 1024 * 1024

TM_PROJ = 256
TQ = 128
KEY_CHUNK = 512
STRIP = 16
TS_LRU = 256
TM_ROUTE = 512
MOE_BLK = 256
FILL_ROWS = MOE_BLK + 8
TM_DISP = 256
TM_COMB = 128
TF_UP = 1024


def _cparams(vmem_mib, n_axes):
    return pltpu.CompilerParams(
        vmem_limit_bytes=int(vmem_mib * MIB),
        dimension_semantics=("arbitrary",) * n_axes,
    )


def _dot(a, b):
    return jnp.dot(a, b, preferred_element_type=f32)


def _dot_nt(a, b):
    return lax.dot_general(a, b, (((1,), (1,)), ((), ())), preferred_element_type=f32)


def _split_bf16(x):
    hi = x.astype(bf16)
    lo = (x - hi.astype(f32)).astype(bf16)
    return hi, lo


def _group_meansq(x, group):
    w = x.shape[1]
    r = lax.broadcasted_iota(i32, (w, w), 0) // group
    c = lax.broadcasted_iota(i32, (w, w), 1) // group
    ones_bd = jnp.where(r == c, 1.0, 0.0).astype(bf16)
    hi, lo = _split_bf16(x * x)
    return (_dot(hi, ones_bd) + _dot(lo, ones_bd)) * (1.0 / group)


def _rms_rows(x, gain):
    ms = jnp.mean(x * x, axis=-1, keepdims=True)
    return x * lax.rsqrt(ms + NORM_EPS) * gain


def _masked_softmax(s, m):
    sm = jnp.where(m, s, NEG_INF)
    mx = jnp.max(sm, axis=-1, keepdims=True)
    e = jnp.where(m, jnp.exp(sm - mx), 0.0)
    den = jnp.sum(e, axis=-1, keepdims=True)
    inv = jnp.where(den > 0.0, 1.0 / den, 0.0)
    return e * inv


def _bias_softmax(sb):
    mx = jnp.max(sb, axis=-1, keepdims=True)
    e = jnp.exp(sb - mx)
    den = jnp.sum(e, axis=-1, keepdims=True)
    return (e * (1.0 / den)).astype(bf16)


def _ada_kernel(c_ref, w_ref, b_ref, o_ref):
    c = c_ref[...]
    sc = c * jax.nn.sigmoid(c)
    o_ref[...] = _dot(sc.astype(bf16), w_ref[...].astype(bf16)) + b_ref[...]


def _ada_mod(c, w_ada, b_ada):
    bsz, d = c.shape
    n = w_ada.shape[1]
    tn = 1024
    return pl.pallas_call(
        _ada_kernel,
        grid=(n // tn,),
        in_specs=[
            pl.BlockSpec((bsz, d), lambda j: (0, 0)),
            pl.BlockSpec((d, tn), lambda j: (0, j)),
            pl.BlockSpec((1, tn), lambda j: (0, j)),
        ],
        out_specs=pl.BlockSpec((bsz, tn), lambda j: (0, j)),
        out_shape=jax.ShapeDtypeStruct((bsz, n), f32),
        compiler_params=_cparams(40, 1),
        name="ada_mod",
    )(c, w_ada, b_ada.reshape(1, n))


def _inproj_kernel(x_ref, g_ref, mod_ref, w_ref, zq_ref, zkv_ref, zgl_ref, zx_ref):
    x = x_ref[...]
    y = _rms_rows(x, g_ref[...])
    h = y * (1.0 + mod_ref[1:2, :]) + mod_ref[0:1, :]
    z = _dot(h.astype(bf16), w_ref[...])
    o = 0
    for ref in (zq_ref, zkv_ref, zgl_ref, zx_ref):
        wdt = ref.shape[1]
        ref[...] = z[:, o:o + wdt]
        o += wdt


def _inproj(xf, g1, mod3, w_pad, seq):
    n, d = xf.shape
    tm = TM_PROJ
    tpb = seq // tm
    widths = (NSA_WIDTH, 6 * KV_WIDTH, GATE_PAD, 2 * LRU_WIDTH)
    return pl.pallas_call(
        _inproj_kernel,
        grid=(n // tm,),
        in_specs=[
            pl.BlockSpec((tm, d), lambda i: (i, 0)),
            pl.BlockSpec((1, d), lambda i: (0, 0)),
            pl.BlockSpec((None, 6, d), lambda i: (i // tpb, 0, 0)),
            pl.BlockSpec((d, IN_PAD), lambda i: (0, 0), pipeline_mode=pl.Buffered(1)),
        ],
        out_specs=[pl.BlockSpec((tm, w), lambda i: (i, 0)) for w in widths],
        out_shape=[jax.ShapeDtypeStruct((n, w), f32) for w in widths],
        compiler_params=_cparams(56, 1),
        name="in_proj",
    )(xf, g1.reshape(1, d), mod3, w_pad)


def _prep_kernel(zq_ref, zkv_ref, zgl_ref, qg_ref, kg_ref, qn_ref, ks_ref, vs_ref, kw_ref, vw_ref, gate_ref,
                 *, tiles_per_seq):
    gw = HEADS_PER_GROUP * HEAD_DIM
    qg = qg_ref[...]
    for g in range(KV_GROUPS):
        xg = zq_ref[:, g * gw:(g + 1) * gw]
        ms = _group_meansq(xg, HEAD_DIM)
        qn = xg * lax.rsqrt(ms + NORM_EPS) * qg * (HEAD_DIM ** -0.5)
        qn_ref[:, g * gw:(g + 1) * gw] = qn.astype(bf16)

    def norm_k(col, row):
        xk = zkv_ref[:, col * KV_WIDTH:(col + 1) * KV_WIDTH]
        ms = _group_meansq(xk, HEAD_DIM)
        return xk * lax.rsqrt(ms + NORM_EPS) * kg_ref[row:row + 1, :]

    ksn = norm_k(2, 1)
    kwn = norm_k(4, 2)
    vs = zkv_ref[:, 3 * KV_WIDTH:4 * KV_WIDTH]
    vw = zkv_ref[:, 5 * KV_WIDTH:6 * KV_WIDTH]
    gl = jax.nn.sigmoid(zgl_ref[...])
    ng = 3 * HEADS_PER_GROUP
    tm = zq_ref.shape[0]
    pos = (pl.program_id(0) % tiles_per_seq) * tm + lax.broadcasted_iota(i32, (tm, HEAD_DIM), 0)
    blk_onehot = jnp.where(jnp.right_shift(pos, 6) == lax.broadcasted_iota(i32, (tm, HEAD_DIM), 1), 1.0, 0.0)
    for g in range(KV_GROUPS):
        sl = slice(g * HEAD_DIM, (g + 1) * HEAD_DIM)
        ks_ref[g] = jnp.concatenate([ksn[:, sl], blk_onehot], axis=1).astype(bf16)
        vs_ref[g] = vs[:, sl].astype(bf16)
        kw_ref[g] = kwn[:, sl].astype(bf16)
        vw_ref[g] = vw[:, sl].astype(bf16)
        gate_ref[g] = gl[:, g * ng:(g + 1) * ng]


def _prep(zq, zkv, zgl, q_gain, k_gain, bsz, seq):
    n = zq.shape[0]
    tm = TM_PROJ
    tpb = seq // tm
    qg = jnp.tile(q_gain.reshape(1, HEAD_DIM), (1, HEADS_PER_GROUP))
    kg = jnp.tile(k_gain.reshape(3, HEAD_DIM), (1, KV_GROUPS))
    hm = lambda w: pl.BlockSpec((None, KV_GROUPS, tm, w), lambda i: (i // tpb, 0, i % tpb, 0))
    hshape = lambda w, dt: jax.ShapeDtypeStruct((bsz, KV_GROUPS, seq, w), dt)
    ng = 3 * HEADS_PER_GROUP
    return pl.pallas_call(
        functools.partial(_prep_kernel, tiles_per_seq=tpb),
        grid=(n // tm,),
        in_specs=[
            pl.BlockSpec((tm, NSA_WIDTH), lambda i: (i, 0)),
            pl.BlockSpec((tm, 6 * KV_WIDTH), lambda i: (i, 0)),
            pl.BlockSpec((tm, GATE_PAD), lambda i: (i, 0)),
            pl.BlockSpec((1, HEADS_PER_GROUP * HEAD_DIM), lambda i: (0, 0)),
            pl.BlockSpec((3, KV_WIDTH), lambda i: (0, 0)),
        ],
        out_specs=[
            pl.BlockSpec((tm, NSA_WIDTH), lambda i: (i, 0)),
            hm(2 * HEAD_DIM), hm(HEAD_DIM), hm(HEAD_DIM), hm(HEAD_DIM), hm(ng),
        ],
        out_shape=[
            jax.ShapeDtypeStruct((n, NSA_WIDTH), bf16),
            hshape(2 * HEAD_DIM, bf16), hshape(HEAD_DIM, bf16), hshape(HEAD_DIM, bf16), hshape(HEAD_DIM, bf16),
            hshape(ng, f32),
        ],
        compiler_params=_cparams(32, 1),
        name="nsa_prep",
    )(zq, zkv, zgl, qg, kg)


def _compress_kernel(fk_ref, fv_ref, pek_ref, pev_ref, wk_ref, wv_ref, kg_ref, kc_ref, vc_ref):
    kc = _dot((fk_ref[...] + pek_ref[...]).astype(bf16), wk_ref[...])
    vc = _dot((fv_ref[...] + pev_ref[...]).astype(bf16), wv_ref[...])
    kc_ref[...] = _rms_rows(kc, kg_ref[...]).astype(bf16)
    vc_ref[...] = vc.astype(bf16)


def _flat_blocks(z, bsz, seq):
    nsub = seq // CMP_STRIDE
    sub = z.reshape(bsz, nsub, CMP_STRIDE, KV_GROUPS, HEAD_DIM)
    nxt = jnp.concatenate([sub[:, 1:], jnp.zeros_like(sub[:, :1])], axis=1)
    blocks = jnp.concatenate([sub, nxt], axis=2)
    return blocks.transpose(0, 3, 1, 2, 4).reshape(bsz, KV_GROUPS, nsub, CMP_LEN * HEAD_DIM)


def _compress(zkv, pe_k, pe_v, w_ck, w_cv, k_gain0, bsz, seq):
    nsub = seq // CMP_STRIDE
    fk = _flat_blocks(zkv[:, 0:KV_WIDTH], bsz, seq)
    fv = _flat_blocks(zkv[:, KV_WIDTH:2 * KV_WIDTH], bsz, seq)
    kdim = CMP_LEN * HEAD_DIM
    blk = pl.BlockSpec((None, None, nsub, kdim), lambda b, g: (b, g, 0, 0))
    cst = lambda r, c: pl.BlockSpec((r, c), lambda b, g: (0, 0))
    oblk = pl.BlockSpec((None, None, nsub, HEAD_DIM), lambda b, g: (b, g, 0, 0))
    oshape = jax.ShapeDtypeStruct((bsz, KV_GROUPS, nsub, HEAD_DIM), bf16)
    return pl.pallas_call(
        _compress_kernel,
        grid=(bsz, KV_GROUPS),
        in_specs=[blk, blk, cst(1, kdim), cst(1, kdim), cst(kdim, HEAD_DIM), cst(kdim, HEAD_DIM), cst(1, HEAD_DIM)],
        out_specs=[oblk, oblk],
        out_shape=[oshape, oshape],
        compiler_params=_cparams(32, 2),
        name="nsa_compress",
    )(fk, fv, pe_k.reshape(1, kdim), pe_v.reshape(1, kdim), w_ck.astype(bf16), w_cv.astype(bf16),
      k_gain0.reshape(1, HEAD_DIM))


def _softmax_strips(s_ref, p_ref, r_ref, width, mask_from, key_ok):
    strips = [slice(i * STRIP, (i + 1) * STRIP) for i in range(s_ref.shape[0] // STRIP)]
    wide = lambda v: jnp.concatenate([v] * (width // LANES), axis=1)
    lanes = lambda v: jnp.broadcast_to(v, (STRIP, LANES))

    for r in strips:
        tail = jnp.where(key_ok(r.start % TQ, mask_from, width - mask_from), s_ref[r, mask_from:width], NEG_INF)
        s_ref[r, mask_from:width] = tail
        m = jnp.max(tail, axis=-1, keepdims=True)
        if mask_from:
            m = jnp.maximum(m, jnp.max(s_ref[r, 0:mask_from], axis=-1, keepdims=True))
        r_ref[r, :] = lanes(m)
    for r in strips:
        e = jnp.exp(s_ref[r, 0:width] - wide(r_ref[r, :]))
        s_ref[r, 0:width] = e
        r_ref[r, :] = lanes(1.0 / jnp.sum(e, axis=-1, keepdims=True))
    for r in strips:
        p_ref[r, 0:width] = (s_ref[r, 0:width] * wide(r_ref[r, :])).astype(bf16)


def _attn_kernel(q_ref, kc_ref, vc_ref, ks_ref, vs_ref, kw_ref, vw_ref, gate_ref, o_ref,
                 os_ref, s_ref, p_ref, r_ref, *, seq):
    t = pl.program_id(2)
    hg = HEADS_PER_GROUP
    q = q_ref[...]
    qs = jnp.concatenate([q[:, h * HEAD_DIM:(h + 1) * HEAD_DIM] for h in range(hg)], axis=0)
    tpos4 = t * TQ + lax.rem(lax.broadcasted_iota(i32, (hg * TQ, 1), 0), TQ)
    t0 = t * TQ
    halves = [slice(0, hg // 2 * TQ), slice(hg // 2 * TQ, hg * TQ)]

    ncmp = kc_ref.shape[0]
    nblk = seq // SEL_LEN
    s_c = _dot_nt(qs, kc_ref[...])
    cstart = lax.broadcasted_iota(i32, (1, ncmp), 1) * CMP_STRIDE
    p_c = _masked_softmax(s_c, (cstart + (CMP_LEN - 1)) <= tpos4).astype(bf16)
    o_c = _dot(p_c, vc_ref[...])

    cs = lax.broadcasted_iota(i32, (nblk, ncmp), 1) * CMP_STRIDE
    ss = lax.broadcasted_iota(i32, (nblk, ncmp), 0) * SEL_LEN
    overlap_t = jnp.where((cs < ss + SEL_LEN) & (cs + CMP_LEN > ss), 1.0, 0.0).astype(bf16)
    imp4 = _dot_nt(overlap_t, p_c)
    imp = imp4[:, 0:TQ]
    for h in range(1, hg):
        imp = imp + imp4[:, h * TQ:(h + 1) * TQ]
    j = lax.broadcasted_iota(i32, (nblk, TQ), 0)
    qblk = jnp.right_shift(t * TQ + lax.broadcasted_iota(i32, (nblk, TQ), 1), 6)
    forced = (j == 0) | (j == qblk) | (j == qblk - 1)
    impf = jnp.where(forced, SEL_FORCE, jnp.where(j <= qblk, imp, -SEL_FORCE))
    beaten = jnp.zeros((nblk, TQ), i32)
    for i in range(nblk):
        ci = impf[i:i + 1, :]
        beats = (ci > impf) | ((ci == impf) & (j > i))
        beaten = beaten + jnp.where(beats, 1, 0)
    keep = (beaten < min(N_SEL, nblk)) & (j <= qblk)
    sel_bias = jnp.where(keep, 0.0, NEG_INF)
    if nblk < HEAD_DIM:
        sel_bias = jnp.concatenate([sel_bias, jnp.zeros((HEAD_DIM - nblk, TQ), f32)], axis=0)
    sel_bias = sel_bias.T.astype(bf16)
    q_aug = jnp.concatenate(
        [jnp.concatenate([q[:, h * HEAD_DIM:(h + 1) * HEAD_DIM], sel_bias], axis=1) for h in range(hg)], axis=0)

    def causal_ok(tok0, col0, ncols):
        tpos_s = t0 + tok0 + lax.broadcasted_iota(i32, (STRIP, 1), 0)
        return (col0 + lax.broadcasted_iota(i32, (1, ncols), 1)) <= tpos_s

    n_chunks = (t0 + TQ + KEY_CHUNK - 1) // KEY_CHUNK
    for c in range(seq // KEY_CHUNK):

        @pl.when(n_chunks == c + 1)
        def _(width=(c + 1) * KEY_CHUNK):
            for rows in halves:
                s_ref[rows, 0:width] = _dot_nt(q_aug[rows], ks_ref[0:width, :])
            _softmax_strips(s_ref, p_ref, r_ref, width, width - KEY_CHUNK, causal_ok)
            for rows in halves:
                os_ref[rows, :] = _dot(p_ref[rows, 0:width], vs_ref[0:width, :])

    o_s = os_ref[...]

    wk = WINDOW + TQ
    start = pl.multiple_of(jnp.maximum(t0 - WINDOW, 0), TQ)

    def window_ok(tok0, col0, ncols):
        tpos_s = t0 + tok0 + lax.broadcasted_iota(i32, (STRIP, 1), 0)
        wpos = start + col0 + lax.broadcasted_iota(i32, (1, ncols), 1)
        return (wpos <= tpos_s) & (wpos > tpos_s - WINDOW)

    for rows in halves:
        s_ref[rows, 0:wk] = _dot_nt(qs[rows], kw_ref[pl.ds(start, wk), :])
    _softmax_strips(s_ref, p_ref, r_ref, wk, 0, window_ok)
    o_w = jnp.concatenate([_dot(p_ref[rows, 0:wk], vw_ref[pl.ds(start, wk), :]) for rows in halves], axis=0)

    gts = gate_ref[...]
    outs = []
    for h in range(hg):
        rows = slice(h * TQ, (h + 1) * TQ)
        outs.append(gts[:, 3 * h:3 * h + 1] * o_c[rows] + gts[:, 3 * h + 1:3 * h + 2] * o_s[rows]
                    + gts[:, 3 * h + 2:3 * h + 3] * o_w[rows])
    o_ref[...] = jnp.concatenate(outs, axis=1)


def _attention(qn, kc, vc, ks, vs, kw, vw, gates, bsz, seq):
    n = qn.shape[0]
    gw = HEADS_PER_GROUP * HEAD_DIM
    ntq = seq // TQ
    nsub = seq // CMP_STRIDE
    ng = 3 * HEADS_PER_GROUP
    qblk = pl.BlockSpec((TQ, gw), lambda b, g, t: (b * ntq + t, g))
    full = lambda r, w: pl.BlockSpec((None, None, r, w), lambda b, g, t: (b, g, 0, 0))
    return pl.pallas_call(
        functools.partial(_attn_kernel, seq=seq),
        grid=(bsz, KV_GROUPS, ntq),
        in_specs=[
            qblk, full(nsub, HEAD_DIM), full(nsub, HEAD_DIM),
            full(seq, 2 * HEAD_DIM), full(seq, HEAD_DIM), full(seq, HEAD_DIM), full(seq, HEAD_DIM),
            pl.BlockSpec((None, None, TQ, ng), lambda b, g, t: (b, g, t, 0)),
        ],
        out_specs=qblk,
        out_shape=jax.ShapeDtypeStruct((n, NSA_WIDTH), f32),
        scratch_shapes=[
            pltpu.VMEM((HEADS_PER_GROUP * TQ, HEAD_DIM), f32),
            pltpu.VMEM((HEADS_PER_GROUP * TQ, seq), f32),
            pltpu.VMEM((HEADS_PER_GROUP * TQ, seq), bf16),
            pltpu.VMEM((HEADS_PER_GROUP * TQ, LANES), f32),
        ],
        compiler_params=_cparams(48, 3),
        name="nsa_attention",
    )(qn, kc, vc, ks, vs, kw, vw, gates)


def _lru_kernel(xr_ref, xg_ref, cw_ref, cb_ref, wr_ref, br_ref, wi_ref, bi_ref, lam_ref, o_ref,
                xbuf, hstate, abuf, ubuf):
    ts = xr_ref.shape[0]
    s = pl.program_id(1)

    @pl.when(s == 0)
    def _():
        xbuf[0:8, :] = jnp.zeros((8, LRU_WIDTH), f32)
        hstate[...] = jnp.zeros_like(hstate)

    xbuf[8:8 + ts, :] = xr_ref[...]
    acc = xbuf[5:5 + ts, :] * cw_ref[0:1, :]
    for k in range(1, CONV_W):
        acc = acc + xbuf[5 + k:5 + k + ts, :] * cw_ref[k:k + 1, :]
    xc = cb_ref[...] + acc
    xbuf[0:8, :] = xbuf[ts:ts + 8, :]

    xcb = xc.astype(bf16)
    r = jax.nn.sigmoid(_dot(xcb, wr_ref[...]) + br_ref[...])
    ig = jax.nn.sigmoid(_dot(xcb, wi_ref[...]) + bi_ref[...])
    nl = -lam_ref[...]
    softplus = jnp.maximum(nl, 0.0) + jnp.log1p(jnp.exp(-jnp.abs(nl)))
    log_a = (-RG_C * softplus) * r
    abuf[...] = jnp.exp(log_a)
    th = jnp.tanh(log_a)
    ubuf[...] = jnp.sqrt(-2.0 * th / (1.0 - th)) * (ig * xc)

    def step(jb, h):
        base = pl.multiple_of(jb * 8, 8)
        a8 = abuf[pl.ds(base, 8), :]
        u8 = ubuf[pl.ds(base, 8), :]
        rows = []
        for rr in range(8):
            h = a8[rr:rr + 1, :] * h + u8[rr:rr + 1, :]
            rows.append(h)
        ubuf[pl.ds(base, 8), :] = jnp.concatenate(rows, axis=0)
        return h

    hstate[0:1, :] = lax.fori_loop(0, ts // 8, step, hstate[0:1, :])
    xg = xg_ref[...]
    gelu = 0.5 * xg * (1.0 + jnp.tanh(0.7978845608028654 * (xg + 0.044715 * (xg * xg * xg))))
    o_ref[...] = ubuf[...] * gelu


def _block_diag(w):
    nb, bw, _ = w.shape
    eye = jnp.eye(nb, dtype=w.dtype)
    return (w[:, :, None, :] * eye[:, None, :, None]).reshape(nb * bw, nb * bw)


def _lru(zx, conv_w, conv_b, w_rg, b_rg, w_ig, b_ig, lam, bsz, seq):
    n = zx.shape[0]
    ts = TS_LRU
    tpb = seq // ts
    w = LRU_WIDTH
    row = lambda v: v.reshape(1, w)
    cst = lambda r, c: pl.BlockSpec((r, c), lambda b, s: (0, 0))
    return pl.pallas_call(
        _lru_kernel,
        grid=(bsz, tpb),
        in_specs=[
            pl.BlockSpec((ts, w), lambda b, s: (b * tpb + s, 0)),
            pl.BlockSpec((ts, w), lambda b, s: (b * tpb + s, 1)),
            cst(CONV_W, w), cst(1, w), cst(w, w), cst(1, w), cst(w, w), cst(1, w), cst(1, w),
        ],
        out_specs=pl.BlockSpec((ts, w), lambda b, s: (b * tpb + s, 0)),
        out_shape=jax.ShapeDtypeStruct((n, w), f32),
        scratch_shapes=[
            pltpu.VMEM((ts + 8, w), f32), pltpu.VMEM((8, w), f32),
            pltpu.VMEM((ts, w), f32), pltpu.VMEM((ts, w), f32),
        ],
        compiler_params=_cparams(40, 2),
        name="rglru",
    )(zx, zx, conv_w, row(conv_b), _block_diag(w_rg).astype(bf16), row(b_rg),
      _block_diag(w_ig).astype(bf16), row(b_ig), row(lam))


def _outproj_kernel(on_ref, ol_ref, x_ref, mod_ref, gn_ref, gl_ref, w_ref, g2_ref, wrh_ref, wrl_ref, br_ref,
                    x1_ref, hp_ref, te_ref, tg_ref):
    nn = _rms_rows(on_ref[...], gn_ref[...])
    nl = _rms_rows(ol_ref[...], gl_ref[...])
    hcat = jnp.concatenate([nn, nl], axis=1).astype(bf16)
    mix = _dot(hcat, w_ref[...])
    x1 = x_ref[...] + mod_ref[2:3, :] * mix
    x1_ref[...] = x1
    h2 = _rms_rows(x1, g2_ref[...]) * (1.0 + mod_ref[4:5, :]) + mod_ref[3:4, :]

    half = D_MODEL // 2
    hb = h2.astype(bf16).astype(f32)
    hi_bits = lax.bitcast_convert_type(hb[:, :half], i32)
    lo_bits = lax.shift_right_logical(lax.bitcast_convert_type(hb[:, half:], i32), 16)
    hp_ref[...] = hi_bits | lo_bits

    hh, hl = _split_bf16(h2)
    logits = _dot(hh, wrh_ref[...]) + _dot(hl, wrh_ref[...]) + _dot(hh, wrl_ref[...]) + br_ref[...]
    lane = lax.broadcasted_iota(i32, (1, LANES), 1)
    lane_f = lane.astype(f32)
    vals, idxs = [], []
    cur = logits
    for _ in range(TOP_K):
        m = jnp.max(cur, axis=-1, keepdims=True)
        idx = jnp.min(jnp.where(cur == m, lane_f, float(LANES)), axis=-1, keepdims=True).astype(i32)
        vals.append(m)
        idxs.append(idx)
        cur = jnp.where(lane == idx, -3e38, cur)
    es = [jnp.exp(v - vals[0]) for v in vals]
    den = es[0]
    for e in es[1:]:
        den = den + e
    inv = 1.0 / den
    te = jnp.full(logits.shape, -1, i32)
    tg = jnp.zeros(logits.shape, f32)
    for k in range(TOP_K):
        te = jnp.where(lane == k, idxs[k], te)
        tg = jnp.where(lane == k, es[k] * inv, tg)
    te_ref[...] = te
    tg_ref[...] = tg


def _outproj(o_nsa, o_lru, xf, mod3, g_out_nsa, g_out_lru, w_out, g_norm2, w_router, b_router, seq):
    n, d = xf.shape
    tm = TM_PROJ
    tpb = seq // tm
    wr = jnp.zeros((d, LANES), f32).at[:, :N_EXPERTS].set(w_router)
    wrh = wr.astype(bf16)
    wrl = (wr - wrh.astype(f32)).astype(bf16)
    br = jnp.full((1, LANES), NEG_INF, f32).at[0, :N_EXPERTS].set(b_router)
    rows = lambda w: pl.BlockSpec((tm, w), lambda i: (i, 0))
    cst = lambda r, c: pl.BlockSpec((r, c), lambda i: (0, 0))
    return pl.pallas_call(
        _outproj_kernel,
        grid=(n // tm,),
        in_specs=[
            rows(NSA_WIDTH), rows(LRU_WIDTH), rows(d),
            pl.BlockSpec((None, 6, d), lambda i: (i // tpb, 0, 0)),
            cst(1, NSA_WIDTH), cst(1, LRU_WIDTH), cst(d, d), cst(1, d), cst(d, LANES), cst(d, LANES), cst(1, LANES),
        ],
        out_specs=[rows(d), rows(d // 2), rows(LANES), rows(LANES)],
        out_shape=[
            jax.ShapeDtypeStruct((n, d), f32), jax.ShapeDtypeStruct((n, d // 2), i32),
            jax.ShapeDtypeStruct((n, LANES), i32), jax.ShapeDtypeStruct((n, LANES), f32),
        ],
        compiler_params=_cparams(48, 1),
        name="out_proj_router",
    )(o_nsa, o_lru, xf, mod3, g_out_nsa.reshape(1, -1), g_out_lru.reshape(1, -1), w_out.astype(bf16),
      g_norm2.reshape(1, d), wrh, wrl, br)


def _rank_kernel(te_ref, rank_ref, cnt_ref, carry):
    i = pl.program_id(0)
    tm = te_ref.shape[0]

    @pl.when(i == 0)
    def _():
        carry[...] = jnp.zeros_like(carry)

    te = te_ref[...]
    lane = lax.broadcasted_iota(i32, (1, LANES), 1)
    hits = [te[:, k:k + 1] == lane for k in range(TOP_K)]
    onehot = jnp.zeros((tm, LANES), f32)
    for h in hits:
        onehot = onehot + jnp.where(h, 1.0, 0.0)
    r = lax.broadcasted_iota(i32, (tm, tm), 0)
    c = lax.broadcasted_iota(i32, (tm, tm), 1)
    lower = jnp.where(c < r, 1.0, 0.0).astype(bf16)
    prefix = _dot(lower, onehot.astype(bf16)) + carry[0:1, :]
    rank = jnp.zeros((tm, LANES), f32)
    for k in range(TOP_K):
        rk = jnp.sum(jnp.where(hits[k], prefix, 0.0), axis=-1, keepdims=True)
        rank = jnp.where(lane == k, rk, rank)
    rank_ref[...] = rank.astype(i32)
    carry[0:1, :] = carry[0:1, :] + jnp.sum(onehot, axis=0, keepdims=True)
    cnt_ref[...] = carry[...]


def _ranks(te_pad):
    n = te_pad.shape[0]
    tm = TM_ROUTE
    return pl.pallas_call(
        _rank_kernel,
        grid=(n // tm,),
        in_specs=[pl.BlockSpec((tm, LANES), lambda i: (i, 0))],
        out_specs=[pl.BlockSpec((tm, LANES), lambda i: (i, 0)), pl.BlockSpec((8, LANES), lambda i: (0, 0))],
        out_shape=[jax.ShapeDtypeStruct((n, LANES), i32), jax.ShapeDtypeStruct((8, LANES), f32)],
        scratch_shapes=[pltpu.VMEM((8, LANES), f32)],
        compiler_params=_cparams(32, 1),
        name="route_ranks",
    )(te_pad)


def _slot_kernel(te_ref, rank_ref, pstart_ref, dest_ref):
    te = te_ref[...]
    lane = lax.broadcasted_iota(i32, (1, LANES), 1)
    ps = pstart_ref[...].astype(f32)
    dest = jnp.zeros(te.shape, i32)
    for k in range(TOP_K):
        base = jnp.sum(jnp.where(te[:, k:k + 1] == lane, ps, 0.0), axis=-1, keepdims=True)
        dest = jnp.where(lane == k, base.astype(i32), dest)
    dest_ref[...] = dest + rank_ref[...]


def _slots(te_pad, rank_pad, pstart_row):
    n = te_pad.shape[0]
    tm = TM_ROUTE
    blk = pl.BlockSpec((tm, LANES), lambda i: (i, 0))
    return pl.pallas_call(
        _slot_kernel,
        grid=(n // tm,),
        in_specs=[blk, blk, pl.BlockSpec((1, LANES), lambda i: (0, 0))],
        out_specs=blk,
        out_shape=jax.ShapeDtypeStruct((n, LANES), i32),
        compiler_params=_cparams(32, 1),
        name="route_slots",
    )(te_pad, rank_pad, pstart_row)


def _row_copy(src, src_row, dst, dst_row, sem):
    return pltpu.make_async_copy(src.at[pl.ds(src_row, 1)], dst.at[pl.ds(dst_row, 1)], sem)


def _dispatch_kernel(dest_ref, fill_ref, h_ref, xs_ref, zbuf, sem, zsem):
    tm = h_ref.shape[0]

    @pl.when(pl.program_id(0) == 0)
    def _():
        zbuf[...] = jnp.zeros_like(zbuf)

        def fill(e, c):
            start = pl.multiple_of(fill_ref[e], 8)
            cp = pltpu.make_async_copy(zbuf, xs_ref.at[pl.ds(start, FILL_ROWS)], zsem)
            cp.start()
            cp.wait()
            return c

        lax.fori_loop(0, fill_ref.shape[0], fill, 0)

    def issue(i, c):
        for k in range(TOP_K):
            _row_copy(h_ref, i, xs_ref, dest_ref[0, i * TOP_K + k], sem).start(priority=k % 2)
        return c

    lax.fori_loop(0, tm, issue, 0)

    def drain(i, c):
        for k in range(TOP_K):
            _row_copy(h_ref, 0, xs_ref, 0, sem).wait()
        return c

    lax.fori_loop(0, tm, drain, 0)


def _dispatch(dest_tiles, fill_start, hp, n_slots):
    n, w = hp.shape
    tm = TM_DISP
    return pl.pallas_call(
        _dispatch_kernel,
        grid=(n // tm,),
        in_specs=[
            pl.BlockSpec((None, 1, tm * TOP_K), lambda i: (i, 0, 0), memory_space=pltpu.SMEM),
            pl.BlockSpec(memory_space=pltpu.SMEM),
            pl.BlockSpec((tm, w), lambda i: (i, 0)),
        ],
        out_specs=pl.BlockSpec(memory_space=pl.ANY),
        out_shape=jax.ShapeDtypeStruct((n_slots, w), i32),
        scratch_shapes=[pltpu.VMEM((FILL_ROWS, w), i32), pltpu.SemaphoreType.DMA(()), pltpu.SemaphoreType.DMA(())],
        compiler_params=_cparams(32, 1),
        name="moe_dispatch",
    )(dest_tiles, fill_start, hp)


def _unpack_rows(xp):
    hi = lax.bitcast_convert_type(xp & jnp.int32(-65536), f32)
    lo = lax.bitcast_convert_type(lax.shift_left(xp, 16), f32)
    return jnp.concatenate([hi, lo], axis=1).astype(bf16)


def _stage_weights(i, n_live, be_ref, nxt_ref, slot_ref, copies, cast):
    first = (i < n_live) & ((i == 0) | (be_ref[i] != be_ref[jnp.maximum(i - 1, 0)]))
    slot = slot_ref[i]

    @pl.when(first & (i == 0))
    def _():
        for cp in copies(be_ref[i], slot):
            cp.start()

    @pl.when(first)
    def _():
        for cp in copies(be_ref[i], slot):
            cp.wait()

        @pl.when(nxt_ref[i] >= 0)
        def _():
            for cp in copies(nxt_ref[i], 1 - slot):
                cp.start()

        cast(slot)


def _up_kernel(be_ref, nb_ref, nxt_ref, slot_ref, xs_ref, w_hbm, bg_ref, bl_ref, act_ref, stage, wgb, wlb, sem):
    n = pl.program_id(0)
    i = pl.program_id(1)
    live = i < nb_ref[0]
    tf = wgb.shape[1]

    def copies(e, s):
        cols = [pl.ds(pl.multiple_of(half * D_FF + n * tf, LANES), tf) for half in range(2)]
        return [pltpu.make_async_copy(w_hbm.at[e, :, cols[half]], stage.at[s, half], sem.at[s]) for half in range(2)]

    def cast(s):
        wgb[...] = stage[s, 0].astype(bf16)
        wlb[...] = stage[s, 1].astype(bf16)

    _stage_weights(i, nb_ref[0], be_ref, nxt_ref, slot_ref, copies, cast)

    @pl.when(live)
    def _():
        x = _unpack_rows(xs_ref[...])
        ug = _dot(x, wgb[...]) + bg_ref[...]
        ul = _dot(x, wlb[...]) + bl_ref[...]
        ug = jnp.minimum(ug, SWIGLU_LIMIT)
        ul = jnp.clip(ul, -SWIGLU_LIMIT, SWIGLU_LIMIT)
        act_ref[...] = (ug * jax.nn.sigmoid(SWIGLU_ALPHA * ug) * (ul + 1.0)).astype(bf16)

    @pl.when(pl.program_id(1) >= nb_ref[0])
    def _():
        act_ref[...] = jnp.zeros_like(act_ref)


def _expert_up(route, xs, w_e1, b_e1):
    n_slots, w = xs.shape
    nblk = n_slots // MOE_BLK
    tf = TF_UP
    ncol = D_FF // tf
    d = D_MODEL
    grid_spec = pltpu.PrefetchScalarGridSpec(
        num_scalar_prefetch=4,
        grid=(ncol, nblk),
        in_specs=[
            pl.BlockSpec((MOE_BLK, w), lambda n, i, be, nb, nx, sl: (jnp.minimum(i, nb[0] - 1), 0)),
            pl.BlockSpec(memory_space=pl.ANY),
            pl.BlockSpec((None, 1, tf), lambda n, i, be, nb, nx, sl: (be[i], 0, n)),
            pl.BlockSpec((None, 1, tf), lambda n, i, be, nb, nx, sl: (be[i], 0, ncol + n)),
        ],
        out_specs=pl.BlockSpec((MOE_BLK, tf), lambda n, i, be, nb, nx, sl: (i, n)),
        scratch_shapes=[
            pltpu.VMEM((2, 2, d, tf), f32), pltpu.VMEM((d, tf), bf16), pltpu.VMEM((d, tf), bf16),
            pltpu.SemaphoreType.DMA((2,)),
        ],
    )
    return pl.pallas_call(
        _up_kernel,
        grid_spec=grid_spec,
        out_shape=jax.ShapeDtypeStruct((n_slots, D_FF), bf16),
        compiler_params=_cparams(56, 2),
        name="moe_up",
    )(*route, xs, w_e1, b_e1.reshape(N_EXPERTS, 1, 2 * D_FF), b_e1.reshape(N_EXPERTS, 1, 2 * D_FF))


def _down_kernel(be_ref, nb_ref, nxt_ref, slot_ref, act_ref, w_hbm, b_ref, y_ref, stage, wb, sem):
    i = pl.program_id(1)
    live = i < nb_ref[0]

    def copies(e, s):
        return [pltpu.make_async_copy(w_hbm.at[e], stage.at[s], sem.at[s])]

    def cast(s):
        wb[...] = stage[s].astype(bf16)

    _stage_weights(i, nb_ref[0], be_ref, nxt_ref, slot_ref, copies, cast)

    @pl.when(live)
    def _():
        y_ref[...] = _dot(act_ref[...], wb[...]) + b_ref[...]

    @pl.when(pl.program_id(1) >= nb_ref[0])
    def _():
        y_ref[...] = jnp.zeros_like(y_ref)


def _expert_down(route, act, w_e2, b_e2):
    n_slots = act.shape[0]
    nblk = n_slots // MOE_BLK
    grid_spec = pltpu.PrefetchScalarGridSpec(
        num_scalar_prefetch=4,
        grid=(1, nblk),
        in_specs=[
            pl.BlockSpec((MOE_BLK, D_FF), lambda n, i, be, nb, nx, sl: (i, 0)),
            pl.BlockSpec(memory_space=pl.ANY),
            pl.BlockSpec((None, 1, D_MODEL), lambda n, i, be, nb, nx, sl: (be[i], 0, 0)),
        ],
        out_specs=pl.BlockSpec((MOE_BLK, D_MODEL), lambda n, i, be, nb, nx, sl: (i, 0)),
        scratch_shapes=[
            pltpu.VMEM((2, D_FF, D_MODEL), f32), pltpu.VMEM((D_FF, D_MODEL), bf16), pltpu.SemaphoreType.DMA((2,)),
        ],
    )
    return pl.pallas_call(
        _down_kernel,
        grid_spec=grid_spec,
        out_shape=jax.ShapeDtypeStruct((n_slots, D_MODEL), f32),
        compiler_params=_cparams(56, 2),
        name="moe_down",
    )(*route, act, w_e2, b_e2.reshape(N_EXPERTS, 1, D_MODEL))


def _combine_kernel(dest_ref, y_ref, x1_ref, tg_ref, mod_ref, o_ref, buf, sem):
    tm = x1_ref.shape[0]

    def issue(i, c):
        for k in range(TOP_K):
            _row_copy(y_ref, dest_ref[0, i * TOP_K + k], buf.at[k], i, sem).start(priority=k % 2)
        return c

    lax.fori_loop(0, tm, issue, 0)

    def drain(i, c):
        for k in range(TOP_K):
            _row_copy(y_ref, 0, buf.at[k], 0, sem).wait()
        return c

    lax.fori_loop(0, tm, drain, 0)
    tg = tg_ref[...]
    acc = tg[:, 0:1] * buf[0]
    for k in range(1, TOP_K):
        acc = acc + tg[:, k:k + 1] * buf[k]
    o_ref[...] = x1_ref[...] + mod_ref[5:6, :] * acc


def _combine(dest_tiles, y, x1, tg_pad, mod3, seq):
    n, d = x1.shape
    tm = TM_COMB
    tpb = seq // tm
    return pl.pallas_call(
        _combine_kernel,
        grid=(n // tm,),
        in_specs=[
            pl.BlockSpec((None, 1, tm * TOP_K), lambda i: (i, 0, 0), memory_space=pltpu.SMEM),
            pl.BlockSpec(memory_space=pl.ANY),
            pl.BlockSpec((tm, d), lambda i: (i, 0)),
            pl.BlockSpec((tm, LANES), lambda i: (i, 0)),
            pl.BlockSpec((None, 6, d), lambda i: (i // tpb, 0, 0)),
        ],
        out_specs=pl.BlockSpec((tm, d), lambda i: (i, 0)),
        out_shape=jax.ShapeDtypeStruct((n, d), f32),
        scratch_shapes=[pltpu.VMEM((TOP_K, tm, d), f32), pltpu.SemaphoreType.DMA(())],
        compiler_params=_cparams(32, 1),
        name="moe_combine",
    )(dest_tiles, y, x1, tg_pad, mod3)


def _layer(x, mod, g_norm1, w_in, pe_cmp_k, pe_cmp_v, w_cmp_k, w_cmp_v, q_gain, k_gain, conv_w, conv_b,
           w_rg, b_rg, w_ig, b_ig, lru_lambda, g_out_nsa, g_out_lru, w_out, g_norm2, w_router, b_router,
           w_e1, b_e1, w_e2, b_e2):
    bsz, seq, d = x.shape
    n = bsz * seq
    xf = x.reshape(n, d)
    mod3 = mod.reshape(bsz, 6, d)

    gate_col = NSA_WIDTH + 6 * KV_WIDTH
    n_gate = 3 * N_HEADS
    w_pad = jnp.concatenate(
        [w_in[:, :gate_col + n_gate], jnp.zeros((d, GATE_PAD - n_gate), w_in.dtype), w_in[:, gate_col + n_gate:]],
        axis=1).astype(bf16)
    zq, zkv, zgl, zx = _inproj(xf, g_norm1, mod3, w_pad, seq)

    qn, ks, vs, kw, vw, gates = _prep(zq, zkv, zgl, q_gain, k_gain, bsz, seq)
    kc, vc = _compress(zkv, pe_cmp_k, pe_cmp_v, w_cmp_k, w_cmp_v, k_gain[0], bsz, seq)
    o_nsa = _attention(qn, kc, vc, ks, vs, kw, vw, gates, bsz, seq)
    o_lru = _lru(zx, conv_w, conv_b, w_rg, b_rg, w_ig, b_ig, lru_lambda, bsz, seq)

    x1, hp, te_pad, tg_pad = _outproj(o_nsa, o_lru, xf, mod3, g_out_nsa, g_out_lru, w_out, g_norm2,
                                      w_router, b_router, seq)

    rank_pad, cnt = _ranks(te_pad)
    counts = cnt[0, :N_EXPERTS].astype(i32)
    pcounts = (counts + MOE_BLK - 1) // MOE_BLK * MOE_BLK
    pends = jnp.cumsum(pcounts)
    pstarts = pends - pcounts
    n_blocks = (n * TOP_K + N_EXPERTS * (MOE_BLK - 1) + MOE_BLK - 1) // MOE_BLK
    n_slots = n_blocks * MOE_BLK
    blk_start = jnp.arange(n_blocks, dtype=i32) * MOE_BLK
    blk_e = jnp.minimum(jnp.sum((pends[None, :] <= blk_start[:, None]).astype(i32), axis=1), N_EXPERTS - 1)
    n_used = (pends[-1] // MOE_BLK).astype(i32).reshape(1)
    n_tail = (n_slots - n * TOP_K + MOE_BLK - 1) // MOE_BLK
    fill_rows = jnp.concatenate([pstarts + counts, pends[-1] + jnp.arange(n_tail, dtype=i32) * MOE_BLK])
    fill_start = jnp.minimum(fill_rows // 8 * 8, n_slots - FILL_ROWS).astype(i32)
    pstart_row = jnp.zeros((1, LANES), i32).at[0, :N_EXPERTS].set(pstarts.astype(i32))
    dest_pad = _slots(te_pad, rank_pad, pstart_row)
    dest = dest_pad[:, :TOP_K]

    xs = _dispatch(dest.reshape(n // TM_DISP, 1, TM_DISP * TOP_K), fill_start, hp, n_slots)
    run_first = jnp.concatenate([jnp.ones((1,), bool), blk_e[1:] != blk_e[:-1]])
    slot = ((jnp.cumsum(run_first.astype(i32)) - 1) % 2).astype(i32)
    later = lax.cummin(jnp.where(counts > 0, jnp.arange(N_EXPERTS, dtype=i32), N_EXPERTS), reverse=True)
    nxt_e = jnp.concatenate([later[1:], jnp.full((1,), N_EXPERTS, i32)])
    nxt = jnp.where(nxt_e < N_EXPERTS, nxt_e, -1)[blk_e].astype(i32)
    route = (blk_e, n_used, nxt, slot)
    act = _expert_up(route, xs, w_e1, b_e1)
    y = _expert_down(route, act, w_e2, b_e2)
    out = _combine(dest.reshape(n // TM_COMB, 1, TM_COMB * TOP_K), y, x1, tg_pad, mod3, seq)
    return out.reshape(bsz, seq, d)


def kernel(x, c, w_ada, b_ada, g_norm1, w_in, pe_cmp_k, pe_cmp_v, w_cmp_k, w_cmp_v, q_gain, k_gain, conv_w, conv_b, w_rg, b_rg, w_ig, b_ig, lru_lambda, g_out_nsa, g_out_lru, w_out, g_norm2, w_router, b_router, w_e1, b_e1, w_e2, b_e2):
    for l in range(w_ada.shape[0]):
        mod = _ada_mod(c, w_ada[l], b_ada[l])
        x = _layer(x, mod, g_norm1[l], w_in[l], pe_cmp_k[l], pe_cmp_v[l], w_cmp_k[l], w_cmp_v[l], q_gain[l],
                   k_gain[l], conv_w[l], conv_b[l], w_rg[l], b_rg[l], w_ig[l], b_ig[l], lru_lambda[l],
                   g_out_nsa[l], g_out_lru[l], w_out[l], g_norm2[l], w_router[l], b_router[l], w_e1[l], b_e1[l],
                   w_e2[l], b_e2[l])
    return x
```

```python
import functools

import jax
import jax.numpy as jnp
from jax import lax
from jax.experimental import pallas as pl
from jax.experimental.pallas import tpu as pltpu

f32 = jnp.float32
bf16 = jnp.bfloat16
i32 = jnp.int32

D_MODEL = 2048
N_HEADS = 16
HEAD_DIM = 64
KV_GROUPS = 4
HEADS_PER_GROUP = N_HEADS // KV_GROUPS
NSA_WIDTH = N_HEADS * HEAD_DIM
KV_WIDTH = KV_GROUPS * HEAD_DIM
CMP_LEN = 32
CMP_STRIDE = 16
SEL_LEN = 64
N_SEL = 8
WINDOW = 512
LRU_WIDTH = D_MODEL - NSA_WIDTH
LRU_BLOCKS = 16
CONV_W = 4
RG_C = 8.0
N_EXPERTS = 32
TOP_K = 4
D_FF = D_MODEL
SWIGLU_LIMIT = 7.0
SWIGLU_ALPHA = 1.702
NORM_EPS = 1e-6
NEG_INF = -1e30
SEL_FORCE = 1e30
GATE_PAD = 128
IN_PAD = NSA_WIDTH + 6 * KV_WIDTH + GATE_PAD + 2 * LRU_WIDTH
LANES = 128
MIB = 1024 * 1024

TM_PROJ = 256
TQ = 128
TQ_SELECT = 512
KEY_CHUNK = 512
STRIP = 16
TS_LRU = 256
TM_ROUTE = 512
MOE_BLK = 256
FILL_ROWS = MOE_BLK + 8
TM_DISP = 256
TM_COMB = 128
TF_UP = 1024


def _cparams(vmem_mib, n_axes):
    return pltpu.CompilerParams(
        vmem_limit_bytes=int(vmem_mib * MIB),
        dimension_semantics=("arbitrary",) * n_axes,
    )


def _dot(a, b):
    return jnp.dot(a, b, preferred_element_type=f32)


def _dot_nt(a, b):
    return lax.dot_general(a, b, (((1,), (1,)), ((), ())), preferred_element_type=f32)


def _split_bf16(x):
    hi = x.astype(bf16)
    lo = (x - hi.astype(f32)).astype(bf16)
    return hi, lo


def _group_meansq(x, group):
    w = x.shape[1]
    r = lax.broadcasted_iota(i32, (w, w), 0) // group
    c = lax.broadcasted_iota(i32, (w, w), 1) // group
    ones_bd = jnp.where(r == c, 1.0, 0.0).astype(bf16)
    hi, lo = _split_bf16(x * x)
    return (_dot(hi, ones_bd) + _dot(lo, ones_bd)) * (1.0 / group)


def _rms_rows(x, gain):
    ms = jnp.mean(x * x, axis=-1, keepdims=True)
    return x * lax.rsqrt(ms + NORM_EPS) * gain


def _masked_softmax(s, m):
    sm = jnp.where(m, s, NEG_INF)
    mx = jnp.max(sm, axis=-1, keepdims=True)
    e = jnp.where(m, jnp.exp(sm - mx), 0.0)
    den = jnp.sum(e, axis=-1, keepdims=True)
    inv = jnp.where(den > 0.0, 1.0 / den, 0.0)
    return e * inv


def _bias_softmax(sb):
    mx = jnp.max(sb, axis=-1, keepdims=True)
    e = jnp.exp(sb - mx)
    den = jnp.sum(e, axis=-1, keepdims=True)
    return (e * (1.0 / den)).astype(bf16)


def _ada_kernel(c_ref, w_ref, b_ref, o_ref):
    c = c_ref[...]
    sc = c * jax.nn.sigmoid(c)
    o_ref[...] = _dot(sc.astype(bf16), w_ref[...].astype(bf16)) + b_ref[...]


def _ada_mod(c, w_ada, b_ada):
    bsz, d = c.shape
    n = w_ada.shape[1]
    tn = 1024
    return pl.pallas_call(
        _ada_kernel,
        grid=(n // tn,),
        in_specs=[
            pl.BlockSpec((bsz, d), lambda j: (0, 0)),
            pl.BlockSpec((d, tn), lambda j: (0, j)),
            pl.BlockSpec((1, tn), lambda j: (0, j)),
        ],
        out_specs=pl.BlockSpec((bsz, tn), lambda j: (0, j)),
        out_shape=jax.ShapeDtypeStruct((bsz, n), f32),
        compiler_params=_cparams(40, 1),
        name="ada_mod",
    )(c, w_ada, b_ada.reshape(1, n))


def _inproj_kernel(x_ref, g_ref, mod_ref, w_ref, zq_ref, zkv_ref, zgl_ref, zx_ref):
    x = x_ref[...]
    y = _rms_rows(x, g_ref[...])
    h = y * (1.0 + mod_ref[1:2, :]) + mod_ref[0:1, :]
    z = _dot(h.astype(bf16), w_ref[...])
    o = 0
    for ref in (zq_ref, zkv_ref, zgl_ref, zx_ref):
        wdt = ref.shape[1]
        ref[...] = z[:, o:o + wdt]
        o += wdt


def _inproj(xf, g1, mod3, w_pad, seq):
    n, d = xf.shape
    tm = TM_PROJ
    tpb = seq // tm
    widths = (NSA_WIDTH, 6 * KV_WIDTH, GATE_PAD, 2 * LRU_WIDTH)
    return pl.pallas_call(
        _inproj_kernel,
        grid=(n // tm,),
        in_specs=[
            pl.BlockSpec((tm, d), lambda i: (i, 0)),
            pl.BlockSpec((1, d), lambda i: (0, 0)),
            pl.BlockSpec((None, 6, d), lambda i: (i // tpb, 0, 0)),
            pl.BlockSpec((d, IN_PAD), lambda i: (0, 0), pipeline_mode=pl.Buffered(1)),
        ],
        out_specs=[pl.BlockSpec((tm, w), lambda i: (i, 0)) for w in widths],
        out_shape=[jax.ShapeDtypeStruct((n, w), f32) for w in widths],
        compiler_params=_cparams(56, 1),
        name="in_proj",
    )(xf, g1.reshape(1, d), mod3, w_pad)


def _prep_kernel(zq_ref, zkv_ref, zgl_ref, qg_ref, kg_ref, qn_ref, ks_ref, vs_ref, kw_ref, vw_ref, gate_ref,
                 *, tiles_per_seq):
    gw = HEADS_PER_GROUP * HEAD_DIM
    qg = qg_ref[...]
    for g in range(KV_GROUPS):
        xg = zq_ref[:, g * gw:(g + 1) * gw]
        ms = _group_meansq(xg, HEAD_DIM)
        qn = xg * lax.rsqrt(ms + NORM_EPS) * qg * (HEAD_DIM ** -0.5)
        qn_ref[:, g * gw:(g + 1) * gw] = qn.astype(bf16)

    def norm_k(col, row):
        xk = zkv_ref[:, col * KV_WIDTH:(col + 1) * KV_WIDTH]
        ms = _group_meansq(xk, HEAD_DIM)
        return xk * lax.rsqrt(ms + NORM_EPS) * kg_ref[row:row + 1, :]

    ksn = norm_k(2, 1)
    kwn = norm_k(4, 2)
    vs = zkv_ref[:, 3 * KV_WIDTH:4 * KV_WIDTH]
    vw = zkv_ref[:, 5 * KV_WIDTH:6 * KV_WIDTH]
    gl = jax.nn.sigmoid(zgl_ref[...])
    ng = 3 * HEADS_PER_GROUP
    tm = zq_ref.shape[0]
    pos = (pl.program_id(0) % tiles_per_seq) * tm + lax.broadcasted_iota(i32, (tm, HEAD_DIM), 0)
    blk_onehot = jnp.where(jnp.right_shift(pos, 6) == lax.broadcasted_iota(i32, (tm, HEAD_DIM), 1), 1.0, 0.0)
    for g in range(KV_GROUPS):
        sl = slice(g * HEAD_DIM, (g + 1) * HEAD_DIM)
        ks_ref[g] = jnp.concatenate([ksn[:, sl], blk_onehot], axis=1).astype(bf16)
        vs_ref[g] = vs[:, sl].astype(bf16)
        kw_ref[g] = kwn[:, sl].astype(bf16)
        vw_ref[g] = vw[:, sl].astype(bf16)
        gate_ref[g] = gl[:, g * ng:(g + 1) * ng]


def _prep(zq, zkv, zgl, q_gain, k_gain, bsz, seq):
    n = zq.shape[0]
    tm = TM_PROJ
    tpb = seq // tm
    qg = jnp.tile(q_gain.reshape(1, HEAD_DIM), (1, HEADS_PER_GROUP))
    kg = jnp.tile(k_gain.reshape(3, HEAD_DIM), (1, KV_GROUPS))
    hm = lambda w: pl.BlockSpec((None, KV_GROUPS, tm, w), lambda i: (i // tpb, 0, i % tpb, 0))
    hshape = lambda w, dt: jax.ShapeDtypeStruct((bsz, KV_GROUPS, seq, w), dt)
    ng = 3 * HEADS_PER_GROUP
    return pl.pallas_call(
        functools.partial(_prep_kernel, tiles_per_seq=tpb),
        grid=(n // tm,),
        in_specs=[
            pl.BlockSpec((tm, NSA_WIDTH), lambda i: (i, 0)),
            pl.BlockSpec((tm, 6 * KV_WIDTH), lambda i: (i, 0)),
            pl.BlockSpec((tm, GATE_PAD), lambda i: (i, 0)),
            pl.BlockSpec((1, HEADS_PER_GROUP * HEAD_DIM), lambda i: (0, 0)),
            pl.BlockSpec((3, KV_WIDTH), lambda i: (0, 0)),
        ],
        out_specs=[
            pl.BlockSpec((tm, NSA_WIDTH), lambda i: (i, 0)),
            hm(2 * HEAD_DIM), hm(HEAD_DIM), hm(HEAD_DIM), hm(HEAD_DIM), hm(ng),
        ],
        out_shape=[
            jax.ShapeDtypeStruct((n, NSA_WIDTH), bf16),
            hshape(2 * HEAD_DIM, bf16), hshape(HEAD_DIM, bf16), hshape(HEAD_DIM, bf16), hshape(HEAD_DIM, bf16),
            hshape(ng, f32),
        ],
        compiler_params=_cparams(32, 1),
        name="nsa_prep",
    )(zq, zkv, zgl, qg, kg)


def _compress_kernel(fk_ref, fv_ref, pek_ref, pev_ref, wk_ref, wv_ref, kg_ref, kc_ref, vc_ref):
    def blocks(f_ref, pe_ref, w_ref):
        x = f_ref[...]
        nxt = pltpu.roll(x, x.shape[0] - 1, axis=0)
        half = x.shape[1]
        lo = (x + pe_ref[:, 0:half]).astype(bf16)
        hi = (nxt + pe_ref[:, half:]).astype(bf16)
        return _dot(lo, w_ref[0:half, :]) + _dot(hi, w_ref[half:, :])

    kc_ref[...] = _rms_rows(blocks(fk_ref, pek_ref, wk_ref), kg_ref[...]).astype(bf16)
    vc_ref[...] = blocks(fv_ref, pev_ref, wv_ref).astype(bf16)


def _compress(zkv, pe_k, pe_v, w_ck, w_cv, k_gain0, bsz, seq):
    nsub = seq // CMP_STRIDE
    kdim = CMP_LEN * HEAD_DIM
    sub = zkv[:, 0:2 * KV_WIDTH].reshape(bsz, nsub, CMP_STRIDE, 2, KV_GROUPS, HEAD_DIM)
    sub = sub.transpose(3, 0, 4, 1, 2, 5).reshape(2, bsz, KV_GROUPS, nsub, kdim // 2)
    blk = lambda kv: pl.BlockSpec((None, None, None, nsub, kdim // 2), lambda b, g: (kv, b, g, 0, 0))
    cst = lambda r, c: pl.BlockSpec((r, c), lambda b, g: (0, 0))
    oblk = pl.BlockSpec((None, None, nsub, HEAD_DIM), lambda b, g: (b, g, 0, 0))
    oshape = jax.ShapeDtypeStruct((bsz, KV_GROUPS, nsub, HEAD_DIM), bf16)
    return pl.pallas_call(
        _compress_kernel,
        grid=(bsz, KV_GROUPS),
        in_specs=[blk(0), blk(1), cst(1, kdim), cst(1, kdim), cst(kdim, HEAD_DIM), cst(kdim, HEAD_DIM),
                  cst(1, HEAD_DIM)],
        out_specs=[oblk, oblk],
        out_shape=[oshape, oshape],
        compiler_params=_cparams(32, 2),
        name="nsa_compress",
    )(sub, sub, pe_k.reshape(1, kdim), pe_v.reshape(1, kdim), w_ck.astype(bf16), w_cv.astype(bf16),
      k_gain0.reshape(1, HEAD_DIM))


def _softmax_strips(s_ref, p_ref, r_ref, width, mask_from, key_ok):
    strips = [slice(i * STRIP, (i + 1) * STRIP) for i in range(s_ref.shape[0] // STRIP)]
    wide = lambda v: jnp.concatenate([v] * (width // LANES), axis=1)
    lanes = lambda v: jnp.broadcast_to(v, (STRIP, LANES))

    for r in strips:
        tail = jnp.where(key_ok(r.start % TQ, mask_from, width - mask_from), s_ref[r, mask_from:width], NEG_INF)
        s_ref[r, mask_from:width] = tail
        m = jnp.max(tail, axis=-1, keepdims=True)
        if mask_from:
            m = jnp.maximum(m, jnp.max(s_ref[r, 0:mask_from], axis=-1, keepdims=True))
        r_ref[r, :] = lanes(m)
    for r in strips:
        e = jnp.exp(s_ref[r, 0:width] - wide(r_ref[r, :]))
        s_ref[r, 0:width] = e
        r_ref[r, :] = lanes(1.0 / jnp.sum(e, axis=-1, keepdims=True))
    for r in strips:
        p_ref[r, 0:width] = (s_ref[r, 0:width] * wide(r_ref[r, :])).astype(bf16)


def _select_kernel(q_ref, kc_ref, vc_ref, oc_ref, sb_ref, *, seq):
    tq = q_ref.shape[0]
    t0 = pl.program_id(2) * tq
    hg = HEADS_PER_GROUP
    q = q_ref[...]
    qs = jnp.concatenate([q[:, h * HEAD_DIM:(h + 1) * HEAD_DIM] for h in range(hg)], axis=0)
    tpos4 = t0 + lax.rem(lax.broadcasted_iota(i32, (hg * tq, 1), 0), tq)

    ncmp = kc_ref.shape[0]
    nblk = seq // SEL_LEN
    s_c = _dot_nt(qs, kc_ref[...])
    cstart = lax.broadcasted_iota(i32, (1, ncmp), 1) * CMP_STRIDE
    p_c = _masked_softmax(s_c, (cstart + (CMP_LEN - 1)) <= tpos4).astype(bf16)
    o_c = _dot(p_c, vc_ref[...])
    oc_ref[...] = jnp.concatenate([o_c[h * tq:(h + 1) * tq] for h in range(hg)], axis=1)

    cs = lax.broadcasted_iota(i32, (nblk, ncmp), 1) * CMP_STRIDE
    ss = lax.broadcasted_iota(i32, (nblk, ncmp), 0) * SEL_LEN
    overlap_t = jnp.where((cs < ss + SEL_LEN) & (cs + CMP_LEN > ss), 1.0, 0.0).astype(bf16)
    imp4 = _dot_nt(overlap_t, p_c)
    imp = imp4[:, 0:tq]
    for h in range(1, hg):
        imp = imp + imp4[:, h * tq:(h + 1) * tq]
    j = lax.broadcasted_iota(i32, (nblk, tq), 0)
    qblk = jnp.right_shift(t0 + lax.broadcasted_iota(i32, (nblk, tq), 1), 6)
    forced = (j == 0) | (j == qblk) | (j == qblk - 1)
    impf = jnp.where(forced, SEL_FORCE, jnp.where(j <= qblk, imp, -SEL_FORCE))
    beaten = jnp.zeros((nblk, tq), i32)
    for i in range(nblk):
        ci = impf[i:i + 1, :]
        beats = (ci > impf) | ((ci == impf) & (j > i))
        beaten = beaten + jnp.where(beats, 1, 0)
    keep = (beaten < min(N_SEL, nblk)) & (j <= qblk)
    sel_bias = jnp.where(keep, 0.0, NEG_INF)
    if nblk < HEAD_DIM:
        sel_bias = jnp.concatenate([sel_bias, jnp.zeros((HEAD_DIM - nblk, tq), f32)], axis=0)
    sb_ref[...] = sel_bias.T.astype(bf16)


def _select(qn, kc, vc, bsz, seq):
    n = qn.shape[0]
    gw = HEADS_PER_GROUP * HEAD_DIM
    nt = seq // TQ_SELECT
    nsub = seq // CMP_STRIDE
    qblk = pl.BlockSpec((TQ_SELECT, gw), lambda b, g, t: (b * nt + t, g))
    full = pl.BlockSpec((None, None, nsub, HEAD_DIM), lambda b, g, t: (b, g, 0, 0))
    return pl.pallas_call(
        functools.partial(_select_kernel, seq=seq),
        grid=(bsz, KV_GROUPS, nt),
        in_specs=[qblk, full, full],
        out_specs=[qblk, pl.BlockSpec((None, None, TQ_SELECT, HEAD_DIM), lambda b, g, t: (b, g, t, 0))],
        out_shape=[
            jax.ShapeDtypeStruct((n, NSA_WIDTH), f32),
            jax.ShapeDtypeStruct((bsz, KV_GROUPS, seq, HEAD_DIM), bf16),
        ],
        compiler_params=_cparams(40, 3),
        name="nsa_select",
    )(qn, kc, vc)


def _attn_kernel(q_ref, oc_ref, sb_ref, ks_ref, vs_ref, kw_ref, vw_ref, gate_ref, o_ref,
                 os_ref, s_ref, p_ref, r_ref, *, seq):
    t = pl.program_id(2)
    hg = HEADS_PER_GROUP
    q = q_ref[...]
    qs = jnp.concatenate([q[:, h * HEAD_DIM:(h + 1) * HEAD_DIM] for h in range(hg)], axis=0)
    t0 = t * TQ
    halves = [slice(0, hg // 2 * TQ), slice(hg // 2 * TQ, hg * TQ)]
    sel_bias = sb_ref[...]
    q_aug = jnp.concatenate(
        [jnp.concatenate([q[:, h * HEAD_DIM:(h + 1) * HEAD_DIM], sel_bias], axis=1) for h in range(hg)], axis=0)

    def causal_ok(tok0, col0, ncols):
        tpos_s = t0 + tok0 + lax.broadcasted_iota(i32, (STRIP, 1), 0)
        return (col0 + lax.broadcasted_iota(i32, (1, ncols), 1)) <= tpos_s

    n_chunks = (t0 + TQ + KEY_CHUNK - 1) // KEY_CHUNK
    for c in range(seq // KEY_CHUNK):

        @pl.when(n_chunks == c + 1)
        def _(width=(c + 1) * KEY_CHUNK):
            for rows in halves:
                s_ref[rows, 0:width] = _dot_nt(q_aug[rows], ks_ref[0:width, :])
            _softmax_strips(s_ref, p_ref, r_ref, width, width - KEY_CHUNK, causal_ok)
            for rows in halves:
                os_ref[rows, :] = _dot(p_ref[rows, 0:width], vs_ref[0:width, :])

    o_s = os_ref[...]

    wk = WINDOW + TQ
    start = pl.multiple_of(jnp.maximum(t0 - WINDOW, 0), TQ)

    def window_ok(tok0, col0, ncols):
        tpos_s = t0 + tok0 + lax.broadcasted_iota(i32, (STRIP, 1), 0)
        wpos = start + col0 + lax.broadcasted_iota(i32, (1, ncols), 1)
        return (wpos <= tpos_s) & (wpos > tpos_s - WINDOW)

    for rows in halves:
        s_ref[rows, 0:wk] = _dot_nt(qs[rows], kw_ref[pl.ds(start, wk), :])
    _softmax_strips(s_ref, p_ref, r_ref, wk, 0, window_ok)
    o_w = jnp.concatenate([_dot(p_ref[rows, 0:wk], vw_ref[pl.ds(start, wk), :]) for rows in halves], axis=0)

    gts = gate_ref[...]
    o_c = oc_ref[...]
    outs = []
    for h in range(hg):
        rows = slice(h * TQ, (h + 1) * TQ)
        outs.append(gts[:, 3 * h:3 * h + 1] * o_c[:, h * HEAD_DIM:(h + 1) * HEAD_DIM]
                    + gts[:, 3 * h + 1:3 * h + 2] * o_s[rows] + gts[:, 3 * h + 2:3 * h + 3] * o_w[rows])
    o_ref[...] = jnp.concatenate(outs, axis=1)


def _attention(qn, o_cmp, sel_bias, ks, vs, kw, vw, gates, bsz, seq):
    n = qn.shape[0]
    gw = HEADS_PER_GROUP * HEAD_DIM
    ntq = seq // TQ
    ng = 3 * HEADS_PER_GROUP
    qblk = pl.BlockSpec((TQ, gw), lambda b, g, t: (b * ntq + t, g))
    full = lambda r, w: pl.BlockSpec((None, None, r, w), lambda b, g, t: (b, g, 0, 0))
    return pl.pallas_call(
        functools.partial(_attn_kernel, seq=seq),
        grid=(bsz, KV_GROUPS, ntq),
        in_specs=[
            qblk, qblk, pl.BlockSpec((None, None, TQ, HEAD_DIM), lambda b, g, t: (b, g, t, 0)),
            full(seq, 2 * HEAD_DIM), full(seq, HEAD_DIM), full(seq, HEAD_DIM), full(seq, HEAD_DIM),
            pl.BlockSpec((None, None, TQ, ng), lambda b, g, t: (b, g, t, 0)),
        ],
        out_specs=qblk,
        out_shape=jax.ShapeDtypeStruct((n, NSA_WIDTH), f32),
        scratch_shapes=[
            pltpu.VMEM((HEADS_PER_GROUP * TQ, HEAD_DIM), f32),
            pltpu.VMEM((HEADS_PER_GROUP * TQ, seq), f32),
            pltpu.VMEM((HEADS_PER_GROUP * TQ, seq), bf16),
            pltpu.VMEM((HEADS_PER_GROUP * TQ, LANES), f32),
        ],
        compiler_params=_cparams(48, 3),
        name="nsa_attention",
    )(qn, o_cmp, sel_bias, ks, vs, kw, vw, gates)


def _lru_kernel(xr_ref, xg_ref, cw_ref, cb_ref, wr_ref, br_ref, wi_ref, bi_ref, lam_ref, o_ref,
                xbuf, hstate, abuf, ubuf):
    ts = xr_ref.shape[0]
    s = pl.program_id(1)

    @pl.when(s == 0)
    def _():
        xbuf[0:8, :] = jnp.zeros((8, LRU_WIDTH), f32)
        hstate[...] = jnp.zeros_like(hstate)

    xbuf[8:8 + ts, :] = xr_ref[...]
    acc = xbuf[5:5 + ts, :] * cw_ref[0:1, :]
    for k in range(1, CONV_W):
        acc = acc + xbuf[5 + k:5 + k + ts, :] * cw_ref[k:k + 1, :]
    xc = cb_ref[...] + acc
    xbuf[0:8, :] = xbuf[ts:ts + 8, :]

    xcb = xc.astype(bf16)
    r = jax.nn.sigmoid(_dot(xcb, wr_ref[...]) + br_ref[...])
    ig = jax.nn.sigmoid(_dot(xcb, wi_ref[...]) + bi_ref[...])
    nl = -lam_ref[...]
    softplus = jnp.maximum(nl, 0.0) + jnp.log1p(jnp.exp(-jnp.abs(nl)))
    log_a = (-RG_C * softplus) * r
    abuf[...] = jnp.exp(log_a)
    th = jnp.tanh(log_a)
    ubuf[...] = jnp.sqrt(-2.0 * th / (1.0 - th)) * (ig * xc)

    def step(jb, h):
        base = pl.multiple_of(jb * 8, 8)
        a8 = abuf[pl.ds(base, 8), :]
        u8 = ubuf[pl.ds(base, 8), :]
        rows = []
        for rr in range(8):
            h = a8[rr:rr + 1, :] * h + u8[rr:rr + 1, :]
            rows.append(h)
        ubuf[pl.ds(base, 8), :] = jnp.concatenate(rows, axis=0)
        return h

    hstate[0:1, :] = lax.fori_loop(0, ts // 8, step, hstate[0:1, :])
    xg = xg_ref[...]
    gelu = 0.5 * xg * (1.0 + jnp.tanh(0.7978845608028654 * (xg + 0.044715 * (xg * xg * xg))))
    o_ref[...] = ubuf[...] * gelu


def _block_diag(w):
    nb, bw, _ = w.shape
    n = nb * bw
    tiled = jnp.tile(w.reshape(n, bw), (1, nb))
    same = (lax.broadcasted_iota(i32, (n, n), 0) // bw) == (lax.broadcasted_iota(i32, (n, n), 1) // bw)
    return jnp.where(same, tiled, 0.0)


def _lru(zx, conv_w, conv_b, w_rg, b_rg, w_ig, b_ig, lam, bsz, seq):
    n = zx.shape[0]
    ts = TS_LRU
    tpb = seq // ts
    w = LRU_WIDTH
    row = lambda v: v.reshape(1, w)
    cst = lambda r, c: pl.BlockSpec((r, c), lambda b, s: (0, 0))
    return pl.pallas_call(
        _lru_kernel,
        grid=(bsz, tpb),
        in_specs=[
            pl.BlockSpec((ts, w), lambda b, s: (b * tpb + s, 0)),
            pl.BlockSpec((ts, w), lambda b, s: (b * tpb + s, 1)),
            cst(CONV_W, w), cst(1, w), cst(w, w), cst(1, w), cst(w, w), cst(1, w), cst(1, w),
        ],
        out_specs=pl.BlockSpec((ts, w), lambda b, s: (b * tpb + s, 0)),
        out_shape=jax.ShapeDtypeStruct((n, w), f32),
        scratch_shapes=[
            pltpu.VMEM((ts + 8, w), f32), pltpu.VMEM((8, w), f32),
            pltpu.VMEM((ts, w), f32), pltpu.VMEM((ts, w), f32),
        ],
        compiler_params=_cparams(40, 2),
        name="rglru",
    )(zx, zx, conv_w, row(conv_b), _block_diag(w_rg).astype(bf16), row(b_rg),
      _block_diag(w_ig).astype(bf16), row(b_ig), row(lam))


def _outproj_kernel(on_ref, ol_ref, x_ref, mod_ref, gn_ref, gl_ref, w_ref, g2_ref, wrh_ref, wrl_ref, br_ref,
                    x1_ref, hp_ref, te_ref, tg_ref):
    nn = _rms_rows(on_ref[...], gn_ref[...])
    nl = _rms_rows(ol_ref[...], gl_ref[...])
    hcat = jnp.concatenate([nn, nl], axis=1).astype(bf16)
    mix = _dot(hcat, w_ref[...])
    x1 = x_ref[...] + mod_ref[2:3, :] * mix
    x1_ref[...] = x1
    h2 = _rms_rows(x1, g2_ref[...]) * (1.0 + mod_ref[4:5, :]) + mod_ref[3:4, :]

    half = D_MODEL // 2
    hb = h2.astype(bf16).astype(f32)
    hi_bits = lax.bitcast_convert_type(hb[:, :half], i32)
    lo_bits = lax.shift_right_logical(lax.bitcast_convert_type(hb[:, half:], i32), 16)
    hp_ref[...] = hi_bits | lo_bits

    hh, hl = _split_bf16(h2)
    logits = _dot(hh, wrh_ref[...]) + _dot(hl, wrh_ref[...]) + _dot(hh, wrl_ref[...]) + br_ref[...]
    lane = lax.broadcasted_iota(i32, (1, LANES), 1)
    lane_f = lane.astype(f32)
    vals, idxs = [], []
    cur = logits
    for _ in range(TOP_K):
        m = jnp.max(cur, axis=-1, keepdims=True)
        idx = jnp.min(jnp.where(cur == m, lane_f, float(LANES)), axis=-1, keepdims=True).astype(i32)
        vals.append(m)
        idxs.append(idx)
        cur = jnp.where(lane == idx, -3e38, cur)
    es = [jnp.exp(v - vals[0]) for v in vals]
    den = es[0]
    for e in es[1:]:
        den = den + e
    inv = 1.0 / den
    te = jnp.full(logits.shape, -1, i32)
    tg = jnp.zeros(logits.shape, f32)
    for k in range(TOP_K):
        te = jnp.where(lane == k, idxs[k], te)
        tg = jnp.where(lane == k, es[k] * inv, tg)
    te_ref[...] = te
    tg_ref[...] = tg


def _outproj(o_nsa, o_lru, xf, mod3, g_out_nsa, g_out_lru, w_out, g_norm2, w_router, b_router, seq):
    n, d = xf.shape
    tm = TM_PROJ
    tpb = seq // tm
    wr = jnp.zeros((d, LANES), f32).at[:, :N_EXPERTS].set(w_router)
    wrh = wr.astype(bf16)
    wrl = (wr - wrh.astype(f32)).astype(bf16)
    br = jnp.full((1, LANES), NEG_INF, f32).at[0, :N_EXPERTS].set(b_router)
    rows = lambda w: pl.BlockSpec((tm, w), lambda i: (i, 0))
    cst = lambda r, c: pl.BlockSpec((r, c), lambda i: (0, 0))
    return pl.pallas_call(
        _outproj_kernel,
        grid=(n // tm,),
        in_specs=[
            rows(NSA_WIDTH), rows(LRU_WIDTH), rows(d),
            pl.BlockSpec((None, 6, d), lambda i: (i // tpb, 0, 0)),
            cst(1, NSA_WIDTH), cst(1, LRU_WIDTH), cst(d, d), cst(1, d), cst(d, LANES), cst(d, LANES), cst(1, LANES),
        ],
        out_specs=[rows(d), rows(d // 2), rows(LANES), rows(LANES)],
        out_shape=[
            jax.ShapeDtypeStruct((n, d), f32), jax.ShapeDtypeStruct((n, d // 2), i32),
            jax.ShapeDtypeStruct((n, LANES), i32), jax.ShapeDtypeStruct((n, LANES), f32),
        ],
        compiler_params=_cparams(48, 1),
        name="out_proj_router",
    )(o_nsa, o_lru, xf, mod3, g_out_nsa.reshape(1, -1), g_out_lru.reshape(1, -1), w_out.astype(bf16),
      g_norm2.reshape(1, d), wrh, wrl, br)


def _rank_kernel(te_ref, rank_ref, cnt_ref, carry):
    i = pl.program_id(0)
    tm = te_ref.shape[0]

    @pl.when(i == 0)
    def _():
        carry[...] = jnp.zeros_like(carry)

    te = te_ref[...]
    lane = lax.broadcasted_iota(i32, (1, LANES), 1)
    hits = [te[:, k:k + 1] == lane for k in range(TOP_K)]
    onehot = jnp.zeros((tm, LANES), f32)
    for h in hits:
        onehot = onehot + jnp.where(h, 1.0, 0.0)
    r = lax.broadcasted_iota(i32, (tm, tm), 0)
    c = lax.broadcasted_iota(i32, (tm, tm), 1)
    lower = jnp.where(c < r, 1.0, 0.0).astype(bf16)
    prefix = _dot(lower, onehot.astype(bf16)) + carry[0:1, :]
    rank = jnp.zeros((tm, LANES), f32)
    for k in range(TOP_K):
        rk = jnp.sum(jnp.where(hits[k], prefix, 0.0), axis=-1, keepdims=True)
        rank = jnp.where(lane == k, rk, rank)
    rank_ref[...] = rank.astype(i32)
    carry[0:1, :] = carry[0:1, :] + jnp.sum(onehot, axis=0, keepdims=True)
    cnt_ref[...] = carry[...]


def _ranks(te_pad):
    n = te_pad.shape[0]
    tm = TM_ROUTE
    return pl.pallas_call(
        _rank_kernel,
        grid=(n // tm,),
        in_specs=[pl.BlockSpec((tm, LANES), lambda i: (i, 0))],
        out_specs=[pl.BlockSpec((tm, LANES), lambda i: (i, 0)), pl.BlockSpec((8, LANES), lambda i: (0, 0))],
        out_shape=[jax.ShapeDtypeStruct((n, LANES), i32), jax.ShapeDtypeStruct((8, LANES), f32)],
        scratch_shapes=[pltpu.VMEM((8, LANES), f32)],
        compiler_params=_cparams(32, 1),
        name="route_ranks",
    )(te_pad)


def _slot_kernel(te_ref, rank_ref, pstart_ref, dest_ref):
    te = te_ref[...]
    lane = lax.broadcasted_iota(i32, (1, LANES), 1)
    ps = pstart_ref[...].astype(f32)
    dest = jnp.zeros(te.shape, i32)
    for k in range(TOP_K):
        base = jnp.sum(jnp.where(te[:, k:k + 1] == lane, ps, 0.0), axis=-1, keepdims=True)
        dest = jnp.where(lane == k, base.astype(i32), dest)
    dest_ref[...] = dest + rank_ref[...]


def _slots(te_pad, rank_pad, pstart_row):
    n = te_pad.shape[0]
    tm = TM_ROUTE
    blk = pl.BlockSpec((tm, LANES), lambda i: (i, 0))
    return pl.pallas_call(
        _slot_kernel,
        grid=(n // tm,),
        in_specs=[blk, blk, pl.BlockSpec((1, LANES), lambda i: (0, 0))],
        out_specs=blk,
        out_shape=jax.ShapeDtypeStruct((n, LANES), i32),
        compiler_params=_cparams(32, 1),
        name="route_slots",
    )(te_pad, rank_pad, pstart_row)


def _row_copy(src, src_row, dst, dst_row, sem):
    return pltpu.make_async_copy(src.at[pl.ds(src_row, 1)], dst.at[pl.ds(dst_row, 1)], sem)


def _dispatch_kernel(dest_ref, fill_ref, h_ref, xs_ref, zbuf, sem, zsem):
    tm = h_ref.shape[0]

    @pl.when(pl.program_id(0) == 0)
    def _():
        zbuf[...] = jnp.zeros_like(zbuf)

        def fill(e, c):
            start = pl.multiple_of(fill_ref[e], 8)
            cp = pltpu.make_async_copy(zbuf, xs_ref.at[pl.ds(start, FILL_ROWS)], zsem)
            cp.start()
            cp.wait()
            return c

        lax.fori_loop(0, fill_ref.shape[0], fill, 0)

    def issue(i, c):
        for k in range(TOP_K):
            _row_copy(h_ref, i, xs_ref, dest_ref[0, i * TOP_K + k], sem).start(priority=k % 2)
        return c

    lax.fori_loop(0, tm, issue, 0)

    def drain(i, c):
        for k in range(TOP_K):
            _row_copy(h_ref, 0, xs_ref, 0, sem).wait()
        return c

    lax.fori_loop(0, tm, drain, 0)


def _dispatch(dest_tiles, fill_start, hp, n_slots):
    n, w = hp.shape
    tm = TM_DISP
    return pl.pallas_call(
        _dispatch_kernel,
        grid=(n // tm,),
        in_specs=[
            pl.BlockSpec((None, 1, tm * TOP_K), lambda i: (i, 0, 0), memory_space=pltpu.SMEM),
            pl.BlockSpec(memory_space=pltpu.SMEM),
            pl.BlockSpec((tm, w), lambda i: (i, 0)),
        ],
        out_specs=pl.BlockSpec(memory_space=pl.ANY),
        out_shape=jax.ShapeDtypeStruct((n_slots, w), i32),
        scratch_shapes=[pltpu.VMEM((FILL_ROWS, w), i32), pltpu.SemaphoreType.DMA(()), pltpu.SemaphoreType.DMA(())],
        compiler_params=_cparams(32, 1),
        name="moe_dispatch",
    )(dest_tiles, fill_start, hp)


def _unpack_rows(xp):
    hi = lax.bitcast_convert_type(xp & jnp.int32(-65536), f32)
    lo = lax.bitcast_convert_type(lax.shift_left(xp, 16), f32)
    return jnp.concatenate([hi, lo], axis=1).astype(bf16)


def _stage_weights(i, n_live, be_ref, nxt_ref, slot_ref, copies, cast):
    first = (i < n_live) & ((i == 0) | (be_ref[i] != be_ref[jnp.maximum(i - 1, 0)]))
    slot = slot_ref[i]

    @pl.when(first & (i == 0))
    def _():
        for cp in copies(be_ref[i], slot):
            cp.start()

    @pl.when(first)
    def _():
        for cp in copies(be_ref[i], slot):
            cp.wait()

        @pl.when(nxt_ref[i] >= 0)
        def _():
            for cp in copies(nxt_ref[i], 1 - slot):
                cp.start()

        cast(slot)


def _up_kernel(be_ref, nb_ref, nxt_ref, slot_ref, xs_ref, w_hbm, bg_ref, bl_ref, act_ref, stage, wgb, wlb, sem):
    n = pl.program_id(0)
    i = pl.program_id(1)
    live = i < nb_ref[0]
    tf = wgb.shape[1]

    def copies(e, s):
        cols = [pl.ds(pl.multiple_of(half * D_FF + n * tf, LANES), tf) for half in range(2)]
        return [pltpu.make_async_copy(w_hbm.at[e, :, cols[half]], stage.at[s, half], sem.at[s]) for half in range(2)]

    def cast(s):
        wgb[...] = stage[s, 0].astype(bf16)
        wlb[...] = stage[s, 1].astype(bf16)

    _stage_weights(i, nb_ref[0], be_ref, nxt_ref, slot_ref, copies, cast)

    @pl.when(live)
    def _():
        x = _unpack_rows(xs_ref[...])
        ug = _dot(x, wgb[...]) + bg_ref[...]
        ul = _dot(x, wlb[...]) + bl_ref[...]
        ug = jnp.minimum(ug, SWIGLU_LIMIT)
        ul = jnp.clip(ul, -SWIGLU_LIMIT, SWIGLU_LIMIT)
        act_ref[...] = (ug * jax.nn.sigmoid(SWIGLU_ALPHA * ug) * (ul + 1.0)).astype(bf16)

    @pl.when(pl.program_id(1) >= nb_ref[0])
    def _():
        act_ref[...] = jnp.zeros_like(act_ref)


def _expert_up(route, xs, w_e1, b_e1):
    n_slots, w = xs.shape
    nblk = n_slots // MOE_BLK
    tf = TF_UP
    ncol = D_FF // tf
    d = D_MODEL
    grid_spec = pltpu.PrefetchScalarGridSpec(
        num_scalar_prefetch=4,
        grid=(ncol, nblk),
        in_specs=[
            pl.BlockSpec((MOE_BLK, w), lambda n, i, be, nb, nx, sl: (jnp.minimum(i, nb[0] - 1), 0)),
            pl.BlockSpec(memory_space=pl.ANY),
            pl.BlockSpec((None, 1, tf), lambda n, i, be, nb, nx, sl: (be[i], 0, n)),
            pl.BlockSpec((None, 1, tf), lambda n, i, be, nb, nx, sl: (be[i], 0, ncol + n)),
        ],
        out_specs=pl.BlockSpec((MOE_BLK, tf), lambda n, i, be, nb, nx, sl: (i, n)),
        scratch_shapes=[
            pltpu.VMEM((2, 2, d, tf), f32), pltpu.VMEM((d, tf), bf16), pltpu.VMEM((d, tf), bf16),
            pltpu.SemaphoreType.DMA((2,)),
        ],
    )
    return pl.pallas_call(
        _up_kernel,
        grid_spec=grid_spec,
        out_shape=jax.ShapeDtypeStruct((n_slots, D_FF), bf16),
        compiler_params=_cparams(56, 2),
        name="moe_up",
    )(*route, xs, w_e1, b_e1.reshape(N_EXPERTS, 1, 2 * D_FF), b_e1.reshape(N_EXPERTS, 1, 2 * D_FF))


def _down_kernel(be_ref, nb_ref, nxt_ref, slot_ref, act_ref, w_hbm, b_ref, y_ref, stage, wb, sem):
    i = pl.program_id(1)
    live = i < nb_ref[0]

    def copies(e, s):
        return [pltpu.make_async_copy(w_hbm.at[e], stage.at[s], sem.at[s])]

    def cast(s):
        wb[...] = stage[s].astype(bf16)

    _stage_weights(i, nb_ref[0], be_ref, nxt_ref, slot_ref, copies, cast)

    @pl.when(live)
    def _():
        y_ref[...] = _dot(act_ref[...], wb[...]) + b_ref[...]

    @pl.when(pl.program_id(1) >= nb_ref[0])
    def _():
        y_ref[...] = jnp.zeros_like(y_ref)


def _expert_down(route, act, w_e2, b_e2):
    n_slots = act.shape[0]
    nblk = n_slots // MOE_BLK
    grid_spec = pltpu.PrefetchScalarGridSpec(
        num_scalar_prefetch=4,
        grid=(1, nblk),
        in_specs=[
            pl.BlockSpec((MOE_BLK, D_FF), lambda n, i, be, nb, nx, sl: (i, 0)),
            pl.BlockSpec(memory_space=pl.ANY),
            pl.BlockSpec((None, 1, D_MODEL), lambda n, i, be, nb, nx, sl: (be[i], 0, 0)),
        ],
        out_specs=pl.BlockSpec((MOE_BLK, D_MODEL), lambda n, i, be, nb, nx, sl: (i, 0)),
        scratch_shapes=[
            pltpu.VMEM((2, D_FF, D_MODEL), f32), pltpu.VMEM((D_FF, D_MODEL), bf16), pltpu.SemaphoreType.DMA((2,)),
        ],
    )
    return pl.pallas_call(
        _down_kernel,
        grid_spec=grid_spec,
        out_shape=jax.ShapeDtypeStruct((n_slots, D_MODEL), f32),
        compiler_params=_cparams(56, 2),
        name="moe_down",
    )(*route, act, w_e2, b_e2.reshape(N_EXPERTS, 1, D_MODEL))


def _combine_kernel(dest_ref, dest_next_ref, y_ref, x1_ref, tg_ref, mod_ref, o_ref, buf, sem):
    tm = x1_ref.shape[0]
    step = pl.program_id(0)
    slot = step % 2

    def gather(d_ref, s):
        def issue(i, c):
            for k in range(TOP_K):
                _row_copy(y_ref, d_ref[0, i * TOP_K + k], buf.at[s, k], i, sem.at[s]).start(priority=k % 2)
            return c

        lax.fori_loop(0, tm, issue, 0)

    @pl.when(step == 0)
    def _():
        gather(dest_ref, 0)

    @pl.when(step + 1 < pl.num_programs(0))
    def _():
        gather(dest_next_ref, 1 - slot)

    def drain(i, c):
        for k in range(TOP_K):
            _row_copy(y_ref, 0, buf.at[slot, k], 0, sem.at[slot]).wait()
        return c

    lax.fori_loop(0, tm, drain, 0)
    tg = tg_ref[...]
    acc = tg[:, 0:1] * buf[slot, 0]
    for k in range(1, TOP_K):
        acc = acc + tg[:, k:k + 1] * buf[slot, k]
    o_ref[...] = x1_ref[...] + mod_ref[5:6, :] * acc


def _combine(dest_tiles, y, x1, tg_pad, mod3, seq):
    n, d = x1.shape
    tm = TM_COMB
    tpb = seq // tm
    return pl.pallas_call(
        _combine_kernel,
        grid=(n // tm,),
        in_specs=[
            pl.BlockSpec((None, 1, tm * TOP_K), lambda i: (i, 0, 0), memory_space=pltpu.SMEM),
            pl.BlockSpec((None, 1, tm * TOP_K), lambda i: (jnp.minimum(i + 1, n // tm - 1), 0, 0),
                         memory_space=pltpu.SMEM),
            pl.BlockSpec(memory_space=pl.ANY),
            pl.BlockSpec((tm, d), lambda i: (i, 0)),
            pl.BlockSpec((tm, LANES), lambda i: (i, 0)),
            pl.BlockSpec((None, 6, d), lambda i: (i // tpb, 0, 0)),
        ],
        out_specs=pl.BlockSpec((tm, d), lambda i: (i, 0)),
        out_shape=jax.ShapeDtypeStruct((n, d), f32),
        scratch_shapes=[pltpu.VMEM((2, TOP_K, tm, d), f32), pltpu.SemaphoreType.DMA((2,))],
        compiler_params=_cparams(32, 1),
        name="moe_combine",
    )(dest_tiles, dest_tiles, y, x1, tg_pad, mod3)


def _layer(x, mod, g_norm1, w_in, pe_cmp_k, pe_cmp_v, w_cmp_k, w_cmp_v, q_gain, k_gain, conv_w, conv_b,
           w_rg, b_rg, w_ig, b_ig, lru_lambda, g_out_nsa, g_out_lru, w_out, g_norm2, w_router, b_router,
           w_e1, b_e1, w_e2, b_e2):
    bsz, seq, d = x.shape
    n = bsz * seq
    xf = x.reshape(n, d)
    mod3 = mod.reshape(bsz, 6, d)

    gate_col = NSA_WIDTH + 6 * KV_WIDTH
    n_gate = 3 * N_HEADS
    w_pad = jnp.concatenate(
        [w_in[:, :gate_col + n_gate], jnp.zeros((d, GATE_PAD - n_gate), w_in.dtype), w_in[:, gate_col + n_gate:]],
        axis=1).astype(bf16)
    zq, zkv, zgl, zx = _inproj(xf, g_norm1, mod3, w_pad, seq)

    qn, ks, vs, kw, vw, gates = _prep(zq, zkv, zgl, q_gain, k_gain, bsz, seq)
    kc, vc = _compress(zkv, pe_cmp_k, pe_cmp_v, w_cmp_k, w_cmp_v, k_gain[0], bsz, seq)
    o_cmp, sel_bias = _select(qn, kc, vc, bsz, seq)
    o_nsa = _attention(qn, o_cmp, sel_bias, ks, vs, kw, vw, gates, bsz, seq)
    o_lru = _lru(zx, conv_w, conv_b, w_rg, b_rg, w_ig, b_ig, lru_lambda, bsz, seq)

    x1, hp, te_pad, tg_pad = _outproj(o_nsa, o_lru, xf, mod3, g_out_nsa, g_out_lru, w_out, g_norm2,
                                      w_router, b_router, seq)

    rank_pad, cnt = _ranks(te_pad)
    counts = cnt[0, :N_EXPERTS].astype(i32)
    pcounts = (counts + MOE_BLK - 1) // MOE_BLK * MOE_BLK
    pends = jnp.cumsum(pcounts)
    pstarts = pends - pcounts
    n_blocks = (n * TOP_K + N_EXPERTS * (MOE_BLK - 1) + MOE_BLK - 1) // MOE_BLK
    n_slots = n_blocks * MOE_BLK
    blk_start = jnp.arange(n_blocks, dtype=i32) * MOE_BLK
    blk_e = jnp.minimum(jnp.sum((pends[None, :] <= blk_start[:, None]).astype(i32), axis=1), N_EXPERTS - 1)
    n_used = (pends[-1] // MOE_BLK).astype(i32).reshape(1)
    n_tail = (n_slots - n * TOP_K + MOE_BLK - 1) // MOE_BLK
    fill_rows = jnp.concatenate([pstarts + counts, pends[-1] + jnp.arange(n_tail, dtype=i32) * MOE_BLK])
    fill_start = jnp.minimum(fill_rows // 8 * 8, n_slots - FILL_ROWS).astype(i32)
    pstart_row = jnp.zeros((1, LANES), i32).at[0, :N_EXPERTS].set(pstarts.astype(i32))
    dest_pad = _slots(te_pad, rank_pad, pstart_row)
    dest = dest_pad[:, :TOP_K]

    xs = _dispatch(dest.reshape(n // TM_DISP, 1, TM_DISP * TOP_K), fill_start, hp, n_slots)
    run_first = jnp.concatenate([jnp.ones((1,), bool), blk_e[1:] != blk_e[:-1]])
    slot = ((jnp.cumsum(run_first.astype(i32)) - 1) % 2).astype(i32)
    later = lax.cummin(jnp.where(counts > 0, jnp.arange(N_EXPERTS, dtype=i32), N_EXPERTS), reverse=True)
    nxt_e = jnp.concatenate([later[1:], jnp.full((1,), N_EXPERTS, i32)])
    nxt = jnp.where(nxt_e < N_EXPERTS, nxt_e, -1)[blk_e].astype(i32)
    route = (blk_e, n_used, nxt, slot)
    act = _expert_up(route, xs, w_e1, b_e1)
    y = _expert_down(route, act, w_e2, b_e2)
    out = _combine(dest.reshape(n // TM_COMB, 1, TM_COMB * TOP_K), y, x1, tg_pad, mod3, seq)
    return out.reshape(bsz, seq, d)


def kernel(x, c, w_ada, b_ada, g_norm1, w_in, pe_cmp_k, pe_cmp_v, w_cmp_k, w_cmp_v, q_gain, k_gain, conv_w, conv_b, w_rg, b_rg, w_ig, b_ig, lru_lambda, g_out_nsa, g_out_lru, w_out, g_norm2, w_router, b_router, w_e1, b_e1, w_e2, b_e2):
    for l in range(w_ada.shape[0]):
        mod = _ada_mod(c, w_ada[l], b_ada[l])
        x = _layer(x, mod, g_norm1[l], w_in[l], pe_cmp_k[l], pe_cmp_v[l], w_cmp_k[l], w_cmp_v[l], q_gain[l],
                   k_gain[l], conv_w[l], conv_b[l], w_rg[l], b_rg[l], w_ig[l], b_ig[l], lru_lambda[l],
                   g_out_nsa[l], g_out_lru[l], w_out[l], g_norm2[l], w_router[l], b_router[l], w_e1[l], b_e1[l],
                   w_e2[l], b_e2[l])
    return x
```

```python
import functools

import jax
import jax.numpy as jnp
from jax import lax
from jax.experimental import pallas as pl
from jax.experimental.pallas import tpu as pltpu

f32 = jnp.float32
bf16 = jnp.bfloat16
i32 = jnp.int32

D_MODEL = 2048
N_HEADS = 16
HEAD_DIM = 64
KV_GROUPS = 4
HEADS_PER_GROUP = N_HEADS // KV_GROUPS
NSA_WIDTH = N_HEADS * HEAD_DIM
KV_WIDTH = KV_GROUPS * HEAD_DIM
CMP_LEN = 32
CMP_STRIDE = 16
SEL_LEN = 64
N_SEL = 8
WINDOW = 512
LRU_WIDTH = D_MODEL - NSA_WIDTH
LRU_BLOCKS = 16
CONV_W = 4
RG_C = 8.0
N_EXPERTS = 32
TOP_K = 4
D_FF = D_MODEL
SWIGLU_LIMIT = 7.0
SWIGLU_ALPHA = 1.702
NORM_EPS = 1e-6
NEG_INF = -1e30
SEL_FORCE = 1e30
GATE_PAD = 128
IN_PAD = NSA_WIDTH + 6 * KV_WIDTH + GATE_PAD + 2 * LRU_WIDTH
LANES = 128
MIB = 1024 * 1024

TM_PROJ = 256
TM_OUT = 512
TQ = 128
TQ_SELECT = 512
KEY_CHUNK = 256
STRIP = 16
TS_LRU = 256
TM_ROUTE = 512
MOE_BLK = 256
FILL_ROWS = MOE_BLK + 8
TM_DISP = 256
TM_COMB = 128
TF_UP = 1024


def _cparams(vmem_mib, n_axes):
    return pltpu.CompilerParams(
        vmem_limit_bytes=int(vmem_mib * MIB),
        dimension_semantics=("arbitrary",) * n_axes,
    )


def _dot(a, b):
    return jnp.dot(a, b, preferred_element_type=f32)


def _dot_nt(a, b):
    return lax.dot_general(a, b, (((1,), (1,)), ((), ())), preferred_element_type=f32)


def _split_bf16(x):
    hi = x.astype(bf16)
    lo = (x - hi.astype(f32)).astype(bf16)
    return hi, lo


def _group_meansq(x, group):
    w = x.shape[1]
    r = lax.broadcasted_iota(i32, (w, w), 0) // group
    c = lax.broadcasted_iota(i32, (w, w), 1) // group
    ones_bd = jnp.where(r == c, 1.0, 0.0).astype(bf16)
    hi, lo = _split_bf16(x * x)
    return (_dot(hi, ones_bd) + _dot(lo, ones_bd)) * (1.0 / group)


def _rms_rows(x, gain):
    ms = jnp.mean(x * x, axis=-1, keepdims=True)
    return x * lax.rsqrt(ms + NORM_EPS) * gain


def _masked_softmax(s, m):
    sm = jnp.where(m, s, NEG_INF)
    mx = jnp.max(sm, axis=-1, keepdims=True)
    e = jnp.where(m, jnp.exp(sm - mx), 0.0)
    den = jnp.sum(e, axis=-1, keepdims=True)
    inv = jnp.where(den > 0.0, 1.0 / den, 0.0)
    return e * inv


def _bias_softmax(sb):
    mx = jnp.max(sb, axis=-1, keepdims=True)
    e = jnp.exp(sb - mx)
    den = jnp.sum(e, axis=-1, keepdims=True)
    return (e * (1.0 / den)).astype(bf16)


def _ada_kernel(c_ref, w_ref, b_ref, o_ref):
    c = c_ref[...]
    sc = c * jax.nn.sigmoid(c)
    o_ref[...] = _dot(sc.astype(bf16), w_ref[...].astype(bf16)) + b_ref[...]


def _ada_mod(c, w_ada, b_ada):
    bsz, d = c.shape
    n = w_ada.shape[1]
    tn = 1024
    return pl.pallas_call(
        _ada_kernel,
        grid=(n // tn,),
        in_specs=[
            pl.BlockSpec((bsz, d), lambda j: (0, 0)),
            pl.BlockSpec((d, tn), lambda j: (0, j)),
            pl.BlockSpec((1, tn), lambda j: (0, j)),
        ],
        out_specs=pl.BlockSpec((bsz, tn), lambda j: (0, j)),
        out_shape=jax.ShapeDtypeStruct((bsz, n), f32),
        compiler_params=_cparams(40, 1),
        name="ada_mod",
    )(c, w_ada, b_ada.reshape(1, n))


def _inproj_kernel(x_ref, g_ref, mod_ref, w_ref, qg_ref, kg_ref,
                   qn_ref, ks_ref, vs_ref, kw_ref, vw_ref, gate_ref, zc_ref, zx_ref, *, tiles_per_seq):
    x = x_ref[...]
    y = _rms_rows(x, g_ref[...])
    h = y * (1.0 + mod_ref[1:2, :]) + mod_ref[0:1, :]
    z = _dot(h.astype(bf16), w_ref[...])
    kv0 = NSA_WIDTH
    zc_ref[...] = z[:, kv0:kv0 + 2 * KV_WIDTH]
    zx_ref[...] = z[:, IN_PAD - 2 * LRU_WIDTH:]

    gw = HEADS_PER_GROUP * HEAD_DIM
    qg = qg_ref[...]
    for g in range(KV_GROUPS):
        xg = z[:, g * gw:(g + 1) * gw]
        ms = _group_meansq(xg, HEAD_DIM)
        qn = xg * lax.rsqrt(ms + NORM_EPS) * qg * (HEAD_DIM ** -0.5)
        qn_ref[:, g * gw:(g + 1) * gw] = qn.astype(bf16)

    def kv_part(col):
        return z[:, kv0 + col * KV_WIDTH:kv0 + (col + 1) * KV_WIDTH]

    def norm_k(col, row):
        xk = kv_part(col)
        ms = _group_meansq(xk, HEAD_DIM)
        return xk * lax.rsqrt(ms + NORM_EPS) * kg_ref[row:row + 1, :]

    ksn = norm_k(2, 1)
    kwn = norm_k(4, 2)
    vs = kv_part(3)
    vw = kv_part(5)
    gl = jax.nn.sigmoid(z[:, kv0 + 6 * KV_WIDTH:kv0 + 6 * KV_WIDTH + GATE_PAD])
    ng = 3 * HEADS_PER_GROUP
    tm = x_ref.shape[0]
    pos = (pl.program_id(0) % tiles_per_seq) * tm + lax.broadcasted_iota(i32, (tm, HEAD_DIM), 0)
    blk_onehot = jnp.where(jnp.right_shift(pos, 6) == lax.broadcasted_iota(i32, (tm, HEAD_DIM), 1), 1.0, 0.0)
    for g in range(KV_GROUPS):
        sl = slice(g * HEAD_DIM, (g + 1) * HEAD_DIM)
        ks_ref[g] = jnp.concatenate([ksn[:, sl], blk_onehot], axis=1).astype(bf16)
        vs_ref[g] = vs[:, sl].astype(bf16)
        kw_ref[g] = kwn[:, sl].astype(bf16)
        vw_ref[g] = vw[:, sl].astype(bf16)
        gate_ref[g] = gl[:, g * ng:(g + 1) * ng]


def _inproj(xf, g1, mod3, w_pad, q_gain, k_gain, bsz, seq):
    n, d = xf.shape
    tm = TM_PROJ
    tpb = seq // tm
    qg = jnp.tile(q_gain.reshape(1, HEAD_DIM), (1, HEADS_PER_GROUP))
    kg = jnp.tile(k_gain.reshape(3, HEAD_DIM), (1, KV_GROUPS))
    rows = lambda w: pl.BlockSpec((tm, w), lambda i: (i, 0))
    hm = lambda w: pl.BlockSpec((None, KV_GROUPS, tm, w), lambda i: (i // tpb, 0, i % tpb, 0))
    hshape = lambda w, dt: jax.ShapeDtypeStruct((bsz, KV_GROUPS, seq, w), dt)
    ng = 3 * HEADS_PER_GROUP
    return pl.pallas_call(
        functools.partial(_inproj_kernel, tiles_per_seq=tpb),
        grid=(n // tm,),
        in_specs=[
            rows(d),
            pl.BlockSpec((1, d), lambda i: (0, 0)),
            pl.BlockSpec((None, 6, d), lambda i: (i // tpb, 0, 0)),
            pl.BlockSpec((d, IN_PAD), lambda i: (0, 0), pipeline_mode=pl.Buffered(1)),
            pl.BlockSpec((1, HEADS_PER_GROUP * HEAD_DIM), lambda i: (0, 0)),
            pl.BlockSpec((3, KV_WIDTH), lambda i: (0, 0)),
        ],
        out_specs=[
            rows(NSA_WIDTH), hm(2 * HEAD_DIM), hm(HEAD_DIM), hm(HEAD_DIM), hm(HEAD_DIM), hm(ng),
            rows(2 * KV_WIDTH), rows(2 * LRU_WIDTH),
        ],
        out_shape=[
            jax.ShapeDtypeStruct((n, NSA_WIDTH), bf16),
            hshape(2 * HEAD_DIM, bf16), hshape(HEAD_DIM, bf16), hshape(HEAD_DIM, bf16), hshape(HEAD_DIM, bf16),
            hshape(ng, f32),
            jax.ShapeDtypeStruct((n, 2 * KV_WIDTH), f32), jax.ShapeDtypeStruct((n, 2 * LRU_WIDTH), f32),
        ],
        compiler_params=_cparams(56, 1),
        name="in_proj",
    )(xf, g1.reshape(1, d), mod3, w_pad, qg, kg)


def _compress_kernel(fk_ref, fv_ref, pek_ref, pev_ref, wk_ref, wv_ref, kg_ref, kc_ref, vc_ref):
    def blocks(f_ref, pe_ref, w_ref):
        x = f_ref[...]
        nxt = pltpu.roll(x, x.shape[0] - 1, axis=0)
        half = x.shape[1]
        lo = (x + pe_ref[:, 0:half]).astype(bf16)
        hi = (nxt + pe_ref[:, half:]).astype(bf16)
        return _dot(lo, w_ref[0:half, :]) + _dot(hi, w_ref[half:, :])

    kc_ref[...] = _rms_rows(blocks(fk_ref, pek_ref, wk_ref), kg_ref[...]).astype(bf16)
    vc_ref[...] = blocks(fv_ref, pev_ref, wv_ref).astype(bf16)


def _compress(zkv, pe_k, pe_v, w_ck, w_cv, k_gain0, bsz, seq):
    nsub = seq // CMP_STRIDE
    kdim = CMP_LEN * HEAD_DIM
    sub = zkv[:, 0:2 * KV_WIDTH].reshape(bsz, nsub, CMP_STRIDE, 2, KV_GROUPS, HEAD_DIM)
    sub = sub.transpose(3, 0, 4, 1, 2, 5).reshape(2, bsz, KV_GROUPS, nsub, kdim // 2)
    blk = lambda kv: pl.BlockSpec((None, None, None, nsub, kdim // 2), lambda b, g: (kv, b, g, 0, 0))
    cst = lambda r, c: pl.BlockSpec((r, c), lambda b, g: (0, 0))
    oblk = pl.BlockSpec((None, None, nsub, HEAD_DIM), lambda b, g: (b, g, 0, 0))
    oshape = jax.ShapeDtypeStruct((bsz, KV_GROUPS, nsub, HEAD_DIM), bf16)
    return pl.pallas_call(
        _compress_kernel,
        grid=(bsz, KV_GROUPS),
        in_specs=[blk(0), blk(1), cst(1, kdim), cst(1, kdim), cst(kdim, HEAD_DIM), cst(kdim, HEAD_DIM),
                  cst(1, HEAD_DIM)],
        out_specs=[oblk, oblk],
        out_shape=[oshape, oshape],
        compiler_params=_cparams(32, 2),
        name="nsa_compress",
    )(sub, sub, pe_k.reshape(1, kdim), pe_v.reshape(1, kdim), w_ck.astype(bf16), w_cv.astype(bf16),
      k_gain0.reshape(1, HEAD_DIM))


def _softmax_strips(s_ref, p_ref, r_ref, width, mask_from, key_ok):
    strips = [slice(i * STRIP, (i + 1) * STRIP) for i in range(s_ref.shape[0] // STRIP)]
    wide = lambda v: jnp.concatenate([v] * (width // LANES), axis=1)
    lanes = lambda v: jnp.broadcast_to(v, (STRIP, LANES))

    for r in strips:
        tail = jnp.where(key_ok(r.start % TQ, mask_from, width - mask_from), s_ref[r, mask_from:width], NEG_INF)
        s_ref[r, mask_from:width] = tail
        m = jnp.max(tail, axis=-1, keepdims=True)
        if mask_from:
            m = jnp.maximum(m, jnp.max(s_ref[r, 0:mask_from], axis=-1, keepdims=True))
        r_ref[r, :] = lanes(m)
    for r in strips:
        e = jnp.exp(s_ref[r, 0:width] - wide(r_ref[r, :]))
        s_ref[r, 0:width] = e
        r_ref[r, :] = lanes(1.0 / jnp.sum(e, axis=-1, keepdims=True))
    for r in strips:
        p_ref[r, 0:width] = (s_ref[r, 0:width] * wide(r_ref[r, :])).astype(bf16)


def _select_kernel(q_ref, kc_ref, vc_ref, oc_ref, sb_ref, *, seq):
    tq = q_ref.shape[0]
    t0 = pl.program_id(2) * tq
    hg = HEADS_PER_GROUP
    q = q_ref[...]
    qs = jnp.concatenate([q[:, h * HEAD_DIM:(h + 1) * HEAD_DIM] for h in range(hg)], axis=0)
    tpos4 = t0 + lax.rem(lax.broadcasted_iota(i32, (hg * tq, 1), 0), tq)

    ncmp = kc_ref.shape[0]
    nblk = seq // SEL_LEN
    s_c = _dot_nt(qs, kc_ref[...])
    cstart = lax.broadcasted_iota(i32, (1, ncmp), 1) * CMP_STRIDE
    p_c = _masked_softmax(s_c, (cstart + (CMP_LEN - 1)) <= tpos4).astype(bf16)
    o_c = _dot(p_c, vc_ref[...])
    oc_ref[...] = jnp.concatenate([o_c[h * tq:(h + 1) * tq] for h in range(hg)], axis=1)

    cs = lax.broadcasted_iota(i32, (nblk, ncmp), 1) * CMP_STRIDE
    ss = lax.broadcasted_iota(i32, (nblk, ncmp), 0) * SEL_LEN
    overlap_t = jnp.where((cs < ss + SEL_LEN) & (cs + CMP_LEN > ss), 1.0, 0.0).astype(bf16)
    imp4 = _dot_nt(overlap_t, p_c)
    imp = imp4[:, 0:tq]
    for h in range(1, hg):
        imp = imp + imp4[:, h * tq:(h + 1) * tq]
    j = lax.broadcasted_iota(i32, (nblk, tq), 0)
    qblk = jnp.right_shift(t0 + lax.broadcasted_iota(i32, (nblk, tq), 1), 6)
    forced = (j == 0) | (j == qblk) | (j == qblk - 1)
    impf = jnp.where(forced, SEL_FORCE, jnp.where(j <= qblk, imp, -SEL_FORCE))
    beaten = jnp.zeros((nblk, tq), i32)
    for i in range(nblk):
        ci = impf[i:i + 1, :]
        beats = (ci > impf) | ((ci == impf) & (j > i))
        beaten = beaten + jnp.where(beats, 1, 0)
    keep = (beaten < min(N_SEL, nblk)) & (j <= qblk)
    sel_bias = jnp.where(keep, 0.0, NEG_INF)
    if nblk < HEAD_DIM:
        sel_bias = jnp.concatenate([sel_bias, jnp.zeros((HEAD_DIM - nblk, tq), f32)], axis=0)
    sb_ref[...] = sel_bias.T.astype(bf16)


def _select(qn, kc, vc, bsz, seq):
    n = qn.shape[0]
    gw = HEADS_PER_GROUP * HEAD_DIM
    nt = seq // TQ_SELECT
    nsub = seq // CMP_STRIDE
    qblk = pl.BlockSpec((TQ_SELECT, gw), lambda b, g, t: (b * nt + t, g))
    full = pl.BlockSpec((None, None, nsub, HEAD_DIM), lambda b, g, t: (b, g, 0, 0))
    return pl.pallas_call(
        functools.partial(_select_kernel, seq=seq),
        grid=(bsz, KV_GROUPS, nt),
        in_specs=[qblk, full, full],
        out_specs=[qblk, pl.BlockSpec((None, None, TQ_SELECT, HEAD_DIM), lambda b, g, t: (b, g, t, 0))],
        out_shape=[
            jax.ShapeDtypeStruct((n, NSA_WIDTH), f32),
            jax.ShapeDtypeStruct((bsz, KV_GROUPS, seq, HEAD_DIM), bf16),
        ],
        compiler_params=_cparams(40, 3),
        name="nsa_select",
    )(qn, kc, vc)


def _attn_kernel(q_ref, oc_ref, sb_ref, ks_ref, vs_ref, kw_ref, vw_ref, gate_ref, o_ref,
                 os_ref, s_ref, p_ref, r_ref, *, seq):
    t = pl.program_id(2)
    hg = HEADS_PER_GROUP
    q = q_ref[...]
    qs = jnp.concatenate([q[:, h * HEAD_DIM:(h + 1) * HEAD_DIM] for h in range(hg)], axis=0)
    t0 = t * TQ
    halves = [slice(0, hg // 2 * TQ), slice(hg // 2 * TQ, hg * TQ)]
    sel_bias = sb_ref[...]
    q_aug = jnp.concatenate(
        [jnp.concatenate([q[:, h * HEAD_DIM:(h + 1) * HEAD_DIM], sel_bias], axis=1) for h in range(hg)], axis=0)

    def causal_ok(tok0, col0, ncols):
        tpos_s = t0 + tok0 + lax.broadcasted_iota(i32, (STRIP, 1), 0)
        return (col0 + lax.broadcasted_iota(i32, (1, ncols), 1)) <= tpos_s

    n_chunks = (t0 + TQ + KEY_CHUNK - 1) // KEY_CHUNK
    for c in range(seq // KEY_CHUNK):

        @pl.when(n_chunks == c + 1)
        def _(width=(c + 1) * KEY_CHUNK):
            for rows in halves:
                s_ref[rows, 0:width] = _dot_nt(q_aug[rows], ks_ref[0:width, :])
            _softmax_strips(s_ref, p_ref, r_ref, width, width - KEY_CHUNK, causal_ok)
            for rows in halves:
                os_ref[rows, :] = _dot(p_ref[rows, 0:width], vs_ref[0:width, :])

    o_s = os_ref[...]

    wk = WINDOW + TQ
    start = pl.multiple_of(jnp.maximum(t0 - WINDOW, 0), TQ)

    def window_ok(tok0, col0, ncols):
        tpos_s = t0 + tok0 + lax.broadcasted_iota(i32, (STRIP, 1), 0)
        wpos = start + col0 + lax.broadcasted_iota(i32, (1, ncols), 1)
        return (wpos <= tpos_s) & (wpos > tpos_s - WINDOW)

    for rows in halves:
        s_ref[rows, 0:wk] = _dot_nt(qs[rows], kw_ref[pl.ds(start, wk), :])
    _softmax_strips(s_ref, p_ref, r_ref, wk, 0, window_ok)
    o_w = jnp.concatenate([_dot(p_ref[rows, 0:wk], vw_ref[pl.ds(start, wk), :]) for rows in halves], axis=0)

    gts = gate_ref[...]
    o_c = oc_ref[...]
    outs = []
    for h in range(hg):
        rows = slice(h * TQ, (h + 1) * TQ)
        outs.append(gts[:, 3 * h:3 * h + 1] * o_c[:, h * HEAD_DIM:(h + 1) * HEAD_DIM]
                    + gts[:, 3 * h + 1:3 * h + 2] * o_s[rows] + gts[:, 3 * h + 2:3 * h + 3] * o_w[rows])
    o_ref[...] = jnp.concatenate(outs, axis=1)


def _attention(qn, o_cmp, sel_bias, ks, vs, kw, vw, gates, bsz, seq):
    n = qn.shape[0]
    gw = HEADS_PER_GROUP * HEAD_DIM
    ntq = seq // TQ
    ng = 3 * HEADS_PER_GROUP
    qblk = pl.BlockSpec((TQ, gw), lambda b, g, t: (b * ntq + t, g))
    full = lambda r, w: pl.BlockSpec((None, None, r, w), lambda b, g, t: (b, g, 0, 0))
    return pl.pallas_call(
        functools.partial(_attn_kernel, seq=seq),
        grid=(bsz, KV_GROUPS, ntq),
        in_specs=[
            qblk, qblk, pl.BlockSpec((None, None, TQ, HEAD_DIM), lambda b, g, t: (b, g, t, 0)),
            full(seq, 2 * HEAD_DIM), full(seq, HEAD_DIM), full(seq, HEAD_DIM), full(seq, HEAD_DIM),
            pl.BlockSpec((None, None, TQ, ng), lambda b, g, t: (b, g, t, 0)),
        ],
        out_specs=qblk,
        out_shape=jax.ShapeDtypeStruct((n, NSA_WIDTH), f32),
        scratch_shapes=[
            pltpu.VMEM((HEADS_PER_GROUP * TQ, HEAD_DIM), f32),
            pltpu.VMEM((HEADS_PER_GROUP * TQ, seq), f32),
            pltpu.VMEM((HEADS_PER_GROUP * TQ, seq), bf16),
            pltpu.VMEM((HEADS_PER_GROUP * TQ, LANES), f32),
        ],
        compiler_params=_cparams(48, 3),
        name="nsa_attention",
    )(qn, o_cmp, sel_bias, ks, vs, kw, vw, gates)


def _lru_kernel(xr_ref, xg_ref, cw_ref, cb_ref, wr_ref, br_ref, wi_ref, bi_ref, lam_ref, o_ref,
                xbuf, hstate, abuf, ubuf):
    ts = xr_ref.shape[0]
    s = pl.program_id(1)

    @pl.when(s == 0)
    def _():
        xbuf[0:8, :] = jnp.zeros((8, LRU_WIDTH), f32)
        hstate[...] = jnp.zeros_like(hstate)

    xbuf[8:8 + ts, :] = xr_ref[...]
    acc = xbuf[5:5 + ts, :] * cw_ref[0:1, :]
    for k in range(1, CONV_W):
        acc = acc + xbuf[5 + k:5 + k + ts, :] * cw_ref[k:k + 1, :]
    xc = cb_ref[...] + acc
    xbuf[0:8, :] = xbuf[ts:ts + 8, :]

    xcb = xc.astype(bf16)
    r = jax.nn.sigmoid(_dot(xcb, wr_ref[...]) + br_ref[...])
    ig = jax.nn.sigmoid(_dot(xcb, wi_ref[...]) + bi_ref[...])
    nl = -lam_ref[...]
    softplus = jnp.maximum(nl, 0.0) + jnp.log1p(jnp.exp(-jnp.abs(nl)))
    log_a = (-RG_C * softplus) * r
    abuf[...] = jnp.exp(log_a)
    th = jnp.tanh(log_a)
    ubuf[...] = jnp.sqrt(-2.0 * th / (1.0 - th)) * (ig * xc)

    def step(jb, h):
        base = pl.multiple_of(jb * 8, 8)
        a8 = abuf[pl.ds(base, 8), :]
        u8 = ubuf[pl.ds(base, 8), :]
        rows = []
        for rr in range(8):
            h = a8[rr:rr + 1, :] * h + u8[rr:rr + 1, :]
            rows.append(h)
        ubuf[pl.ds(base, 8), :] = jnp.concatenate(rows, axis=0)
        return h

    hstate[0:1, :] = lax.fori_loop(0, ts // 8, step, hstate[0:1, :])
    xg = xg_ref[...]
    gelu = 0.5 * xg * (1.0 + jnp.tanh(0.7978845608028654 * (xg + 0.044715 * (xg * xg * xg))))
    o_ref[...] = ubuf[...] * gelu


def _block_diag(w):
    nb, bw, _ = w.shape
    n = nb * bw
    tiled = jnp.tile(w.reshape(n, bw), (1, nb))
    same = (lax.broadcasted_iota(i32, (n, n), 0) // bw) == (lax.broadcasted_iota(i32, (n, n), 1) // bw)
    return jnp.where(same, tiled, 0.0)


def _lru(zx, conv_w, conv_b, w_rg, b_rg, w_ig, b_ig, lam, bsz, seq):
    n = zx.shape[0]
    ts = TS_LRU
    tpb = seq // ts
    w = LRU_WIDTH
    row = lambda v: v.reshape(1, w)
    cst = lambda r, c: pl.BlockSpec((r, c), lambda b, s: (0, 0))
    return pl.pallas_call(
        _lru_kernel,
        grid=(bsz, tpb),
        in_specs=[
            pl.BlockSpec((ts, w), lambda b, s: (b * tpb + s, 0)),
            pl.BlockSpec((ts, w), lambda b, s: (b * tpb + s, 1)),
            cst(CONV_W, w), cst(1, w), cst(w, w), cst(1, w), cst(w, w), cst(1, w), cst(1, w),
        ],
        out_specs=pl.BlockSpec((ts, w), lambda b, s: (b * tpb + s, 0)),
        out_shape=jax.ShapeDtypeStruct((n, w), f32),
        scratch_shapes=[
            pltpu.VMEM((ts + 8, w), f32), pltpu.VMEM((8, w), f32),
            pltpu.VMEM((ts, w), f32), pltpu.VMEM((ts, w), f32),
        ],
        compiler_params=_cparams(40, 2),
        name="rglru",
    )(zx, zx, conv_w, row(conv_b), _block_diag(w_rg).astype(bf16), row(b_rg),
      _block_diag(w_ig).astype(bf16), row(b_ig), row(lam))


def _outproj_kernel(on_ref, ol_ref, x_ref, mod_ref, gn_ref, gl_ref, w_ref, g2_ref, wrh_ref, wrl_ref, br_ref,
                    x1_ref, hp_ref, te_ref, tg_ref):
    n_rows = x_ref.shape[0]
    for rows in (slice(0, n_rows // 2), slice(n_rows // 2, n_rows)):
        _outproj_rows(rows, on_ref, ol_ref, x_ref, mod_ref, gn_ref, gl_ref, w_ref, g2_ref, wrh_ref, wrl_ref, br_ref,
                      x1_ref, hp_ref, te_ref, tg_ref)


def _outproj_rows(rows, on_ref, ol_ref, x_ref, mod_ref, gn_ref, gl_ref, w_ref, g2_ref, wrh_ref, wrl_ref, br_ref,
                  x1_ref, hp_ref, te_ref, tg_ref):
    nn = _rms_rows(on_ref[rows, :], gn_ref[...])
    nl = _rms_rows(ol_ref[rows, :], gl_ref[...])
    hcat = jnp.concatenate([nn, nl], axis=1).astype(bf16)
    mix = _dot(hcat, w_ref[...])
    x1 = x_ref[rows, :] + mod_ref[2:3, :] * mix
    x1_ref[rows, :] = x1
    h2 = _rms_rows(x1, g2_ref[...]) * (1.0 + mod_ref[4:5, :]) + mod_ref[3:4, :]

    half = D_MODEL // 2
    hb = h2.astype(bf16).astype(f32)
    hi_bits = lax.bitcast_convert_type(hb[:, :half], i32)
    lo_bits = lax.shift_right_logical(lax.bitcast_convert_type(hb[:, half:], i32), 16)
    hp_ref[rows, :] = hi_bits | lo_bits

    hh, hl = _split_bf16(h2)
    logits = _dot(hh, wrh_ref[...]) + _dot(hl, wrh_ref[...]) + _dot(hh, wrl_ref[...]) + br_ref[...]
    lane = lax.broadcasted_iota(i32, (1, LANES), 1)
    lane_f = lane.astype(f32)
    vals, idxs = [], []
    cur = logits
    for _ in range(TOP_K):
        m = jnp.max(cur, axis=-1, keepdims=True)
        idx = jnp.min(jnp.where(cur == m, lane_f, float(LANES)), axis=-1, keepdims=True).astype(i32)
        vals.append(m)
        idxs.append(idx)
        cur = jnp.where(lane == idx, -3e38, cur)
    es = [jnp.exp(v - vals[0]) for v in vals]
    den = es[0]
    for e in es[1:]:
        den = den + e
    inv = 1.0 / den
    te = jnp.full(logits.shape, -1, i32)
    tg = jnp.zeros(logits.shape, f32)
    for k in range(TOP_K):
        te = jnp.where(lane == k, idxs[k], te)
        tg = jnp.where(lane == k, es[k] * inv, tg)
    te_ref[rows, :] = te
    tg_ref[rows, :] = tg


def _outproj(o_nsa, o_lru, xf, mod3, g_out_nsa, g_out_lru, w_out, g_norm2, w_router, b_router, seq):
    n, d = xf.shape
    tm = TM_OUT
    tpb = seq // tm
    wr = jnp.zeros((d, LANES), f32).at[:, :N_EXPERTS].set(w_router)
    wrh = wr.astype(bf16)
    wrl = (wr - wrh.astype(f32)).astype(bf16)
    br = jnp.full((1, LANES), NEG_INF, f32).at[0, :N_EXPERTS].set(b_router)
    rows = lambda w: pl.BlockSpec((tm, w), lambda i: (i, 0))
    cst = lambda r, c: pl.BlockSpec((r, c), lambda i: (0, 0))
    return pl.pallas_call(
        _outproj_kernel,
        grid=(n // tm,),
        in_specs=[
            rows(NSA_WIDTH), rows(LRU_WIDTH), rows(d),
            pl.BlockSpec((None, 6, d), lambda i: (i // tpb, 0, 0)),
            cst(1, NSA_WIDTH), cst(1, LRU_WIDTH), cst(d, d), cst(1, d), cst(d, LANES), cst(d, LANES), cst(1, LANES),
        ],
        out_specs=[rows(d), rows(d // 2), rows(LANES), rows(LANES)],
        out_shape=[
            jax.ShapeDtypeStruct((n, d), f32), jax.ShapeDtypeStruct((n, d // 2), i32),
            jax.ShapeDtypeStruct((n, LANES), i32), jax.ShapeDtypeStruct((n, LANES), f32),
        ],
        compiler_params=_cparams(56, 1),
        name="out_proj_router",
    )(o_nsa, o_lru, xf, mod3, g_out_nsa.reshape(1, -1), g_out_lru.reshape(1, -1), w_out.astype(bf16),
      g_norm2.reshape(1, d), wrh, wrl, br)


def _rank_kernel(te_ref, rank_ref, cnt_ref, carry):
    i = pl.program_id(0)
    tm = te_ref.shape[0]

    @pl.when(i == 0)
    def _():
        carry[...] = jnp.zeros_like(carry)

    te = te_ref[...]
    lane = lax.broadcasted_iota(i32, (1, LANES), 1)
    hits = [te[:, k:k + 1] == lane for k in range(TOP_K)]
    onehot = jnp.zeros((tm, LANES), f32)
    for h in hits:
        onehot = onehot + jnp.where(h, 1.0, 0.0)
    r = lax.broadcasted_iota(i32, (tm, tm), 0)
    c = lax.broadcasted_iota(i32, (tm, tm), 1)
    lower = jnp.where(c < r, 1.0, 0.0).astype(bf16)
    prefix = _dot(lower, onehot.astype(bf16)) + carry[0:1, :]
    rank = jnp.zeros((tm, LANES), f32)
    for k in range(TOP_K):
        rk = jnp.sum(jnp.where(hits[k], prefix, 0.0), axis=-1, keepdims=True)
        rank = jnp.where(lane == k, rk, rank)
    rank_ref[...] = rank.astype(i32)
    carry[0:1, :] = carry[0:1, :] + jnp.sum(onehot, axis=0, keepdims=True)
    cnt_ref[...] = carry[...]


def _ranks(te_pad):
    n = te_pad.shape[0]
    tm = TM_ROUTE
    return pl.pallas_call(
        _rank_kernel,
        grid=(n // tm,),
        in_specs=[pl.BlockSpec((tm, LANES), lambda i: (i, 0))],
        out_specs=[pl.BlockSpec((tm, LANES), lambda i: (i, 0)), pl.BlockSpec((8, LANES), lambda i: (0, 0))],
        out_shape=[jax.ShapeDtypeStruct((n, LANES), i32), jax.ShapeDtypeStruct((8, LANES), f32)],
        scratch_shapes=[pltpu.VMEM((8, LANES), f32)],
        compiler_params=_cparams(32, 1),
        name="route_ranks",
    )(te_pad)


def _slot_kernel(te_ref, rank_ref, pstart_ref, dest_ref):
    te = te_ref[...]
    lane = lax.broadcasted_iota(i32, (1, LANES), 1)
    ps = pstart_ref[...].astype(f32)
    dest = jnp.zeros(te.shape, i32)
    for k in range(TOP_K):
        base = jnp.sum(jnp.where(te[:, k:k + 1] == lane, ps, 0.0), axis=-1, keepdims=True)
        dest = jnp.where(lane == k, base.astype(i32), dest)
    dest_ref[...] = dest + rank_ref[...]


def _slots(te_pad, rank_pad, pstart_row):
    n = te_pad.shape[0]
    tm = TM_ROUTE
    blk = pl.BlockSpec((tm, LANES), lambda i: (i, 0))
    return pl.pallas_call(
        _slot_kernel,
        grid=(n // tm,),
        in_specs=[blk, blk, pl.BlockSpec((1, LANES), lambda i: (0, 0))],
        out_specs=blk,
        out_shape=jax.ShapeDtypeStruct((n, LANES), i32),
        compiler_params=_cparams(32, 1),
        name="route_slots",
    )(te_pad, rank_pad, pstart_row)


def _row_copy(src, src_row, dst, dst_row, sem):
    return pltpu.make_async_copy(src.at[pl.ds(src_row, 1)], dst.at[pl.ds(dst_row, 1)], sem)


def _dispatch_kernel(dest_ref, fill_ref, h_ref, xs_ref, zbuf, sem, zsem):
    tm = h_ref.shape[0]

    @pl.when(pl.program_id(0) == 0)
    def _():
        zbuf[...] = jnp.zeros_like(zbuf)

        def fill(e, c):
            start = pl.multiple_of(fill_ref[e], 8)
            cp = pltpu.make_async_copy(zbuf, xs_ref.at[pl.ds(start, FILL_ROWS)], zsem)
            cp.start()
            cp.wait()
            return c

        lax.fori_loop(0, fill_ref.shape[0], fill, 0)

    def issue(i, c):
        for k in range(TOP_K):
            _row_copy(h_ref, i, xs_ref, dest_ref[0, i * TOP_K + k], sem).start(priority=k % 2)
        return c

    lax.fori_loop(0, tm, issue, 0)

    def drain(i, c):
        for k in range(TOP_K):
            _row_copy(h_ref, 0, xs_ref, 0, sem).wait()
        return c

    lax.fori_loop(0, tm, drain, 0)


def _dispatch(dest_tiles, fill_start, hp, n_slots):
    n, w = hp.shape
    tm = TM_DISP
    return pl.pallas_call(
        _dispatch_kernel,
        grid=(n // tm,),
        in_specs=[
            pl.BlockSpec((None, 1, tm * TOP_K), lambda i: (i, 0, 0), memory_space=pltpu.SMEM),
            pl.BlockSpec(memory_space=pltpu.SMEM),
            pl.BlockSpec((tm, w), lambda i: (i, 0)),
        ],
        out_specs=pl.BlockSpec(memory_space=pl.ANY),
        out_shape=jax.ShapeDtypeStruct((n_slots, w), i32),
        scratch_shapes=[pltpu.VMEM((FILL_ROWS, w), i32), pltpu.SemaphoreType.DMA(()), pltpu.SemaphoreType.DMA(())],
        compiler_params=_cparams(32, 1),
        name="moe_dispatch",
    )(dest_tiles, fill_start, hp)


def _unpack_rows(xp):
    hi = lax.bitcast_convert_type(xp & jnp.int32(-65536), f32)
    lo = lax.bitcast_convert_type(lax.shift_left(xp, 16), f32)
    return jnp.concatenate([hi, lo], axis=1).astype(bf16)


def _stage_weights(i, n_live, be_ref, nxt_ref, slot_ref, copies, cast):
    first = (i < n_live) & ((i == 0) | (be_ref[i] != be_ref[jnp.maximum(i - 1, 0)]))
    slot = slot_ref[i]

    @pl.when(first & (i == 0))
    def _():
        for cp in copies(be_ref[i], slot):
            cp.start()

    @pl.when(first)
    def _():
        for cp in copies(be_ref[i], slot):
            cp.wait()

        @pl.when(nxt_ref[i] >= 0)
        def _():
            for cp in copies(nxt_ref[i], 1 - slot):
                cp.start()

        cast(slot)


def _up_kernel(be_ref, nb_ref, nxt_ref, slot_ref, xs_ref, w_hbm, bg_ref, bl_ref, act_ref, stage, wgb, wlb, sem):
    n = pl.program_id(0)
    i = pl.program_id(1)
    live = i < nb_ref[0]
    tf = wgb.shape[1]

    def copies(e, s):
        cols = [pl.ds(pl.multiple_of(half * D_FF + n * tf, LANES), tf) for half in range(2)]
        return [pltpu.make_async_copy(w_hbm.at[e, :, cols[half]], stage.at[s, half], sem.at[s]) for half in range(2)]

    def cast(s):
        wgb[...] = stage[s, 0].astype(bf16)
        wlb[...] = stage[s, 1].astype(bf16)

    _stage_weights(i, nb_ref[0], be_ref, nxt_ref, slot_ref, copies, cast)

    @pl.when(live)
    def _():
        x = _unpack_rows(xs_ref[...])
        ug = _dot(x, wgb[...]) + bg_ref[...]
        ul = _dot(x, wlb[...]) + bl_ref[...]
        ug = jnp.minimum(ug, SWIGLU_LIMIT)
        ul = jnp.clip(ul, -SWIGLU_LIMIT, SWIGLU_LIMIT)
        act_ref[...] = (ug * jax.nn.sigmoid(SWIGLU_ALPHA * ug) * (ul + 1.0)).astype(bf16)

    @pl.when(pl.program_id(1) >= nb_ref[0])
    def _():
        act_ref[...] = jnp.zeros_like(act_ref)


def _expert_up(route, xs, w_e1, b_e1):
    n_slots, w = xs.shape
    nblk = n_slots // MOE_BLK
    tf = TF_UP
    ncol = D_FF // tf
    d = D_MODEL
    grid_spec = pltpu.PrefetchScalarGridSpec(
        num_scalar_prefetch=4,
        grid=(ncol, nblk),
        in_specs=[
            pl.BlockSpec((MOE_BLK, w), lambda n, i, be, nb, nx, sl: (jnp.minimum(i, nb[0] - 1), 0)),
            pl.BlockSpec(memory_space=pl.ANY),
            pl.BlockSpec((None, 1, tf), lambda n, i, be, nb, nx, sl: (be[i], 0, n)),
            pl.BlockSpec((None, 1, tf), lambda n, i, be, nb, nx, sl: (be[i], 0, ncol + n)),
        ],
        out_specs=pl.BlockSpec((MOE_BLK, tf), lambda n, i, be, nb, nx, sl: (i, n)),
        scratch_shapes=[
            pltpu.VMEM((2, 2, d, tf), f32), pltpu.VMEM((d, tf), bf16), pltpu.VMEM((d, tf), bf16),
            pltpu.SemaphoreType.DMA((2,)),
        ],
    )
    return pl.pallas_call(
        _up_kernel,
        grid_spec=grid_spec,
        out_shape=jax.ShapeDtypeStruct((n_slots, D_FF), bf16),
        compiler_params=_cparams(56, 2),
        name="moe_up",
    )(*route, xs, w_e1, b_e1.reshape(N_EXPERTS, 1, 2 * D_FF), b_e1.reshape(N_EXPERTS, 1, 2 * D_FF))


def _down_kernel(be_ref, nb_ref, nxt_ref, slot_ref, act_ref, w_hbm, b_ref, y_ref, stage, wb, sem):
    i = pl.program_id(1)
    live = i < nb_ref[0]

    def copies(e, s):
        return [pltpu.make_async_copy(w_hbm.at[e], stage.at[s], sem.at[s])]

    def cast(s):
        wb[...] = stage[s].astype(bf16)

    _stage_weights(i, nb_ref[0], be_ref, nxt_ref, slot_ref, copies, cast)

    @pl.when(live)
    def _():
        y_ref[...] = _dot(act_ref[...], wb[...]) + b_ref[...]

    @pl.when(pl.program_id(1) >= nb_ref[0])
    def _():
        y_ref[...] = jnp.zeros_like(y_ref)


def _expert_down(route, act, w_e2, b_e2):
    n_slots = act.shape[0]
    nblk = n_slots // MOE_BLK
    grid_spec = pltpu.PrefetchScalarGridSpec(
        num_scalar_prefetch=4,
        grid=(1, nblk),
        in_specs=[
            pl.BlockSpec((MOE_BLK, D_FF), lambda n, i, be, nb, nx, sl: (i, 0)),
            pl.BlockSpec(memory_space=pl.ANY),
            pl.BlockSpec((None, 1, D_MODEL), lambda n, i, be, nb, nx, sl: (be[i], 0, 0)),
        ],
        out_specs=pl.BlockSpec((MOE_BLK, D_MODEL), lambda n, i, be, nb, nx, sl: (i, 0)),
        scratch_shapes=[
            pltpu.VMEM((2, D_FF, D_MODEL), f32), pltpu.VMEM((D_FF, D_MODEL), bf16), pltpu.SemaphoreType.DMA((2,)),
        ],
    )
    return pl.pallas_call(
        _down_kernel,
        grid_spec=grid_spec,
        out_shape=jax.ShapeDtypeStruct((n_slots, D_MODEL), f32),
        compiler_params=_cparams(56, 2),
        name="moe_down",
    )(*route, act, w_e2, b_e2.reshape(N_EXPERTS, 1, D_MODEL))


def _combine_kernel(dest_ref, dest_next_ref, y_ref, x1_ref, tg_ref, mod_ref, o_ref, buf, sem):
    tm = x1_ref.shape[0]
    step = pl.program_id(0)
    slot = step % 2

    def gather(d_ref, s):
        def issue(i, c):
            for k in range(TOP_K):
                _row_copy(y_ref, d_ref[0, i * TOP_K + k], buf.at[s, k], i, sem.at[s]).start(priority=k % 2)
            return c

        lax.fori_loop(0, tm, issue, 0)

    @pl.when(step == 0)
    def _():
        gather(dest_ref, 0)

    @pl.when(step + 1 < pl.num_programs(0))
    def _():
        gather(dest_next_ref, 1 - slot)

    def drain(i, c):
        for k in range(TOP_K):
            _row_copy(y_ref, 0, buf.at[slot, k], 0, sem.at[slot]).wait()
        return c

    lax.fori_loop(0, tm, drain, 0)
    tg = tg_ref[...]
    acc = tg[:, 0:1] * buf[slot, 0]
    for k in range(1, TOP_K):
        acc = acc + tg[:, k:k + 1] * buf[slot, k]
    o_ref[...] = x1_ref[...] + mod_ref[5:6, :] * acc


def _combine(dest_tiles, y, x1, tg_pad, mod3, seq):
    n, d = x1.shape
    tm = TM_COMB
    tpb = seq // tm
    return pl.pallas_call(
        _combine_kernel,
        grid=(n // tm,),
        in_specs=[
            pl.BlockSpec((None, 1, tm * TOP_K), lambda i: (i, 0, 0), memory_space=pltpu.SMEM),
            pl.BlockSpec((None, 1, tm * TOP_K), lambda i: (jnp.minimum(i + 1, n // tm - 1), 0, 0),
                         memory_space=pltpu.SMEM),
            pl.BlockSpec(memory_space=pl.ANY),
            pl.BlockSpec((tm, d), lambda i: (i, 0)),
            pl.BlockSpec((tm, LANES), lambda i: (i, 0)),
            pl.BlockSpec((None, 6, d), lambda i: (i // tpb, 0, 0)),
        ],
        out_specs=pl.BlockSpec((tm, d), lambda i: (i, 0)),
        out_shape=jax.ShapeDtypeStruct((n, d), f32),
        scratch_shapes=[pltpu.VMEM((2, TOP_K, tm, d), f32), pltpu.SemaphoreType.DMA((2,))],
        compiler_params=_cparams(32, 1),
        name="moe_combine",
    )(dest_tiles, dest_tiles, y, x1, tg_pad, mod3)


def _layer(x, mod, g_norm1, w_in, pe_cmp_k, pe_cmp_v, w_cmp_k, w_cmp_v, q_gain, k_gain, conv_w, conv_b,
           w_rg, b_rg, w_ig, b_ig, lru_lambda, g_out_nsa, g_out_lru, w_out, g_norm2, w_router, b_router,
           w_e1, b_e1, w_e2, b_e2):
    bsz, seq, d = x.shape
    n = bsz * seq
    xf = x.reshape(n, d)
    mod3 = mod.reshape(bsz, 6, d)

    gate_col = NSA_WIDTH + 6 * KV_WIDTH
    n_gate = 3 * N_HEADS
    w_pad = jnp.concatenate(
        [w_in[:, :gate_col + n_gate], jnp.zeros((d, GATE_PAD - n_gate), w_in.dtype), w_in[:, gate_col + n_gate:]],
        axis=1).astype(bf16)
    qn, ks, vs, kw, vw, gates, zc, zx = _inproj(xf, g_norm1, mod3, w_pad, q_gain, k_gain, bsz, seq)
    kc, vc = _compress(zc, pe_cmp_k, pe_cmp_v, w_cmp_k, w_cmp_v, k_gain[0], bsz, seq)
    o_cmp, sel_bias = _select(qn, kc, vc, bsz, seq)
    o_nsa = _attention(qn, o_cmp, sel_bias, ks, vs, kw, vw, gates, bsz, seq)
    o_lru = _lru(zx, conv_w, conv_b, w_rg, b_rg, w_ig, b_ig, lru_lambda, bsz, seq)

    x1, hp, te_pad, tg_pad = _outproj(o_nsa, o_lru, xf, mod3, g_out_nsa, g_out_lru, w_out, g_norm2,
                                      w_router, b_router, seq)

    rank_pad, cnt = _ranks(te_pad)
    counts = cnt[0, :N_EXPERTS].astype(i32)
    pcounts = (counts + MOE_BLK - 1) // MOE_BLK * MOE_BLK
    pends = jnp.cumsum(pcounts)
    pstarts = pends - pcounts
    n_blocks = (n * TOP_K + N_EXPERTS * (MOE_BLK - 1) + MOE_BLK - 1) // MOE_BLK
    n_slots = n_blocks * MOE_BLK
    blk_start = jnp.arange(n_blocks, dtype=i32) * MOE_BLK
    blk_e = jnp.minimum(jnp.sum((pends[None, :] <= blk_start[:, None]).astype(i32), axis=1), N_EXPERTS - 1)
    n_used = (pends[-1] // MOE_BLK).astype(i32).reshape(1)
    n_tail = (n_slots - n * TOP_K + MOE_BLK - 1) // MOE_BLK
    fill_rows = jnp.concatenate([pstarts + counts, pends[-1] + jnp.arange(n_tail, dtype=i32) * MOE_BLK])
    fill_start = jnp.minimum(fill_rows // 8 * 8, n_slots - FILL_ROWS).astype(i32)
    pstart_row = jnp.zeros((1, LANES), i32).at[0, :N_EXPERTS].set(pstarts.astype(i32))
    dest_pad = _slots(te_pad, rank_pad, pstart_row)
    dest = dest_pad[:, :TOP_K]

    xs = _dispatch(dest.reshape(n // TM_DISP, 1, TM_DISP * TOP_K), fill_start, hp, n_slots)
    run_first = jnp.concatenate([jnp.ones((1,), bool), blk_e[1:] != blk_e[:-1]])
    slot = ((jnp.cumsum(run_first.astype(i32)) - 1) % 2).astype(i32)
    later = lax.cummin(jnp.where(counts > 0, jnp.arange(N_EXPERTS, dtype=i32), N_EXPERTS), reverse=True)
    nxt_e = jnp.concatenate([later[1:], jnp.full((1,), N_EXPERTS, i32)])
    nxt = jnp.where(nxt_e < N_EXPERTS, nxt_e, -1)[blk_e].astype(i32)
    route = (blk_e, n_used, nxt, slot)
    act = _expert_up(route, xs, w_e1, b_e1)
    y = _expert_down(route, act, w_e2, b_e2)
    out = _combine(dest.reshape(n // TM_COMB, 1, TM_COMB * TOP_K), y, x1, tg_pad, mod3, seq)
    return out.reshape(bsz, seq, d)


def kernel(x, c, w_ada, b_ada, g_norm1, w_in, pe_cmp_k, pe_cmp_v, w_cmp_k, w_cmp_v, q_gain, k_gain, conv_w, conv_b, w_rg, b_rg, w_ig, b_ig, lru_lambda, g_out_nsa, g_out_lru, w_out, g_norm2, w_router, b_router, w_e1, b_e1, w_e2, b_e2):
    for l in range(w_ada.shape[0]):
        mod = _ada_mod(c, w_ada[l], b_ada[l])
        x = _layer(x, mod, g_norm1[l], w_in[l], pe_cmp_k[l], pe_cmp_v[l], w_cmp_k[l], w_cmp_v[l], q_gain[l],
                   k_gain[l], conv_w[l], conv_b[l], w_rg[l], b_rg[l], w_ig[l], b_ig[l], lru_lambda[l],
                   g_out_nsa[l], g_out_lru[l], w_out[l], g_norm2[l], w_router[l], b_router[l], w_e1[l], b_e1[l],
                   w_e2[l], b_e2[l])
    return x
```

```python
import functools

import jax
import jax.numpy as jnp
from jax import lax
from jax.experimental import pallas as pl
from jax.experimental.pallas import tpu as pltpu

f32 = jnp.float32
bf16 = jnp.bfloat16
i32 = jnp.int32

D_MODEL = 2048
N_HEADS = 16
HEAD_DIM = 64
KV_GROUPS = 4
HEADS_PER_GROUP = N_HEADS // KV_GROUPS
NSA_WIDTH = N_HEADS * HEAD_DIM
KV_WIDTH = KV_GROUPS * HEAD_DIM
CMP_LEN = 32
CMP_STRIDE = 16
SEL_LEN = 64
N_SEL = 8
WINDOW = 512
LRU_WIDTH = D_MODEL - NSA_WIDTH
LRU_BLOCKS = 16
CONV_W = 4
RG_C = 8.0
N_EXPERTS = 32
TOP_K = 4
D_FF = D_MODEL
SWIGLU_LIMIT = 7.0
SWIGLU_ALPHA = 1.702
NORM_EPS = 1e-6
NEG_INF = -1e30
SEL_FORCE = 1e30
GATE_PAD = 128
IN_PAD = NSA_WIDTH + 6 * KV_WIDTH + GATE_PAD + 2 * LRU_WIDTH
LANES = 128
MIB = 1024 * 1024

TM_PROJ = 256
TM_OUT = 512
TQ = 256
TQ_SELECT = 512
KEY_CHUNK = 256
STRIP = 16
TS_LRU = 256
TM_ROUTE = 512
MOE_BLK = 256
FILL_ROWS = MOE_BLK + 8
TM_DISP = 256
TM_COMB = 128
TF_UP = 1024


def _cparams(vmem_mib, n_axes):
    return pltpu.CompilerParams(
        vmem_limit_bytes=int(vmem_mib * MIB),
        dimension_semantics=("arbitrary",) * n_axes,
    )


def _dot(a, b):
    return jnp.dot(a, b, preferred_element_type=f32)


def _dot_nt(a, b):
    return lax.dot_general(a, b, (((1,), (1,)), ((), ())), preferred_element_type=f32)


def _split_bf16(x):
    hi = x.astype(bf16)
    lo = (x - hi.astype(f32)).astype(bf16)
    return hi, lo


def _group_meansq(x, group):
    w = x.shape[1]
    r = lax.broadcasted_iota(i32, (w, w), 0) // group
    c = lax.broadcasted_iota(i32, (w, w), 1) // group
    ones_bd = jnp.where(r == c, 1.0, 0.0).astype(bf16)
    hi, lo = _split_bf16(x * x)
    return (_dot(hi, ones_bd) + _dot(lo, ones_bd)) * (1.0 / group)


def _rms_rows(x, gain):
    ms = jnp.mean(x * x, axis=-1, keepdims=True)
    return x * lax.rsqrt(ms + NORM_EPS) * gain


def _masked_softmax(s, m):
    sm = jnp.where(m, s, NEG_INF)
    mx = jnp.max(sm, axis=-1, keepdims=True)
    e = jnp.where(m, jnp.exp(sm - mx), 0.0)
    den = jnp.sum(e, axis=-1, keepdims=True)
    inv = jnp.where(den > 0.0, 1.0 / den, 0.0)
    return e * inv


def _bias_softmax(sb):
    mx = jnp.max(sb, axis=-1, keepdims=True)
    e = jnp.exp(sb - mx)
    den = jnp.sum(e, axis=-1, keepdims=True)
    return (e * (1.0 / den)).astype(bf16)


def _ada_kernel(c_ref, w_ref, b_ref, o_ref):
    c = c_ref[...]
    sc = c * jax.nn.sigmoid(c)
    o_ref[...] = _dot(sc.astype(bf16), w_ref[...].astype(bf16)) + b_ref[...]


def _ada_mod(c, w_ada, b_ada):
    bsz, d = c.shape
    n = w_ada.shape[1]
    tn = 1024
    return pl.pallas_call(
        _ada_kernel,
        grid=(n // tn,),
        in_specs=[
            pl.BlockSpec((bsz, d), lambda j: (0, 0)),
            pl.BlockSpec((d, tn), lambda j: (0, j)),
            pl.BlockSpec((1, tn), lambda j: (0, j)),
        ],
        out_specs=pl.BlockSpec((bsz, tn), lambda j: (0, j)),
        out_shape=jax.ShapeDtypeStruct((bsz, n), f32),
        compiler_params=_cparams(40, 1),
        name="ada_mod",
    )(c, w_ada, b_ada.reshape(1, n))


def _inproj_kernel(x_ref, g_ref, mod_ref, w_ref, qg_ref, kg_ref,
                   qn_ref, ks_ref, vs_ref, kw_ref, vw_ref, gate_ref, zc_ref, zx_ref, *, tiles_per_seq):
    x = x_ref[...]
    y = _rms_rows(x, g_ref[...])
    h = y * (1.0 + mod_ref[1:2, :]) + mod_ref[0:1, :]
    z = _dot(h.astype(bf16), w_ref[...])
    kv0 = NSA_WIDTH
    zc_ref[...] = z[:, kv0:kv0 + 2 * KV_WIDTH]
    zx_ref[...] = z[:, IN_PAD - 2 * LRU_WIDTH:]

    gw = HEADS_PER_GROUP * HEAD_DIM
    qg = qg_ref[...]
    for g in range(KV_GROUPS):
        xg = z[:, g * gw:(g + 1) * gw]
        ms = _group_meansq(xg, HEAD_DIM)
        qn = xg * lax.rsqrt(ms + NORM_EPS) * qg * (HEAD_DIM ** -0.5)
        qn_ref[:, g * gw:(g + 1) * gw] = qn.astype(bf16)

    def kv_part(col):
        return z[:, kv0 + col * KV_WIDTH:kv0 + (col + 1) * KV_WIDTH]

    def norm_k(col, row):
        xk = kv_part(col)
        ms = _group_meansq(xk, HEAD_DIM)
        return xk * lax.rsqrt(ms + NORM_EPS) * kg_ref[row:row + 1, :]

    ksn = norm_k(2, 1)
    kwn = norm_k(4, 2)
    vs = kv_part(3)
    vw = kv_part(5)
    gl = jax.nn.sigmoid(z[:, kv0 + 6 * KV_WIDTH:kv0 + 6 * KV_WIDTH + GATE_PAD])
    ng = 3 * HEADS_PER_GROUP
    tm = x_ref.shape[0]
    pos = (pl.program_id(0) % tiles_per_seq) * tm + lax.broadcasted_iota(i32, (tm, HEAD_DIM), 0)
    blk_onehot = jnp.where(jnp.right_shift(pos, 6) == lax.broadcasted_iota(i32, (tm, HEAD_DIM), 1), 1.0, 0.0)
    for g in range(KV_GROUPS):
        sl = slice(g * HEAD_DIM, (g + 1) * HEAD_DIM)
        ks_ref[g] = jnp.concatenate([ksn[:, sl], blk_onehot], axis=1).astype(bf16)
        vs_ref[g] = vs[:, sl].astype(bf16)
        kw_ref[g] = kwn[:, sl].astype(bf16)
        vw_ref[g] = vw[:, sl].astype(bf16)
        gate_ref[g] = gl[:, g * ng:(g + 1) * ng]


def _inproj(xf, g1, mod3, w_pad, q_gain, k_gain, bsz, seq):
    n, d = xf.shape
    tm = TM_PROJ
    tpb = seq // tm
    qg = jnp.tile(q_gain.reshape(1, HEAD_DIM), (1, HEADS_PER_GROUP))
    kg = jnp.tile(k_gain.reshape(3, HEAD_DIM), (1, KV_GROUPS))
    rows = lambda w: pl.BlockSpec((tm, w), lambda i: (i, 0))
    hm = lambda w: pl.BlockSpec((None, KV_GROUPS, tm, w), lambda i: (i // tpb, 0, i % tpb, 0))
    hshape = lambda w, dt: jax.ShapeDtypeStruct((bsz, KV_GROUPS, seq, w), dt)
    ng = 3 * HEADS_PER_GROUP
    return pl.pallas_call(
        functools.partial(_inproj_kernel, tiles_per_seq=tpb),
        grid=(n // tm,),
        in_specs=[
            rows(d),
            pl.BlockSpec((1, d), lambda i: (0, 0)),
            pl.BlockSpec((None, 6, d), lambda i: (i // tpb, 0, 0)),
            pl.BlockSpec((d, IN_PAD), lambda i: (0, 0), pipeline_mode=pl.Buffered(1)),
            pl.BlockSpec((1, HEADS_PER_GROUP * HEAD_DIM), lambda i: (0, 0)),
            pl.BlockSpec((3, KV_WIDTH), lambda i: (0, 0)),
        ],
        out_specs=[
            rows(NSA_WIDTH), hm(2 * HEAD_DIM), hm(HEAD_DIM), hm(HEAD_DIM), hm(HEAD_DIM), hm(ng),
            rows(2 * KV_WIDTH), rows(2 * LRU_WIDTH),
        ],
        out_shape=[
            jax.ShapeDtypeStruct((n, NSA_WIDTH), bf16),
            hshape(2 * HEAD_DIM, bf16), hshape(HEAD_DIM, bf16), hshape(HEAD_DIM, bf16), hshape(HEAD_DIM, bf16),
            hshape(ng, f32),
            jax.ShapeDtypeStruct((n, 2 * KV_WIDTH), f32), jax.ShapeDtypeStruct((n, 2 * LRU_WIDTH), f32),
        ],
        compiler_params=_cparams(56, 1),
        name="in_proj",
    )(xf, g1.reshape(1, d), mod3, w_pad, qg, kg)


def _compress_kernel(fk_ref, fv_ref, pek_ref, pev_ref, wk_ref, wv_ref, kg_ref, kc_ref, vc_ref):
    def blocks(f_ref, pe_ref, w_ref):
        x = f_ref[...]
        nxt = pltpu.roll(x, x.shape[0] - 1, axis=0)
        half = x.shape[1]
        lo = (x + pe_ref[:, 0:half]).astype(bf16)
        hi = (nxt + pe_ref[:, half:]).astype(bf16)
        return _dot(lo, w_ref[0:half, :]) + _dot(hi, w_ref[half:, :])

    kc_ref[...] = _rms_rows(blocks(fk_ref, pek_ref, wk_ref), kg_ref[...]).astype(bf16)
    vc_ref[...] = blocks(fv_ref, pev_ref, wv_ref).astype(bf16)


def _compress(zkv, pe_k, pe_v, w_ck, w_cv, k_gain0, bsz, seq):
    nsub = seq // CMP_STRIDE
    kdim = CMP_LEN * HEAD_DIM
    sub = zkv[:, 0:2 * KV_WIDTH].reshape(bsz, nsub, CMP_STRIDE, 2, KV_GROUPS, HEAD_DIM)
    sub = sub.transpose(3, 0, 4, 1, 2, 5).reshape(2, bsz, KV_GROUPS, nsub, kdim // 2)
    blk = lambda kv: pl.BlockSpec((None, None, None, nsub, kdim // 2), lambda b, g: (kv, b, g, 0, 0))
    cst = lambda r, c: pl.BlockSpec((r, c), lambda b, g: (0, 0))
    oblk = pl.BlockSpec((None, None, nsub, HEAD_DIM), lambda b, g: (b, g, 0, 0))
    oshape = jax.ShapeDtypeStruct((bsz, KV_GROUPS, nsub, HEAD_DIM), bf16)
    return pl.pallas_call(
        _compress_kernel,
        grid=(bsz, KV_GROUPS),
        in_specs=[blk(0), blk(1), cst(1, kdim), cst(1, kdim), cst(kdim, HEAD_DIM), cst(kdim, HEAD_DIM),
                  cst(1, HEAD_DIM)],
        out_specs=[oblk, oblk],
        out_shape=[oshape, oshape],
        compiler_params=_cparams(32, 2),
        name="nsa_compress",
    )(sub, sub, pe_k.reshape(1, kdim), pe_v.reshape(1, kdim), w_ck.astype(bf16), w_cv.astype(bf16),
      k_gain0.reshape(1, HEAD_DIM))


def _softmax_strips(s_ref, p_ref, r_ref, width, mask_from, key_ok):
    strips = [slice(i * STRIP, (i + 1) * STRIP) for i in range(s_ref.shape[0] // STRIP)]
    wide = lambda v: jnp.concatenate([v] * (width // LANES), axis=1)
    lanes = lambda v: jnp.broadcast_to(v, (STRIP, LANES))

    for r in strips:
        tail = jnp.where(key_ok(r.start % TQ, mask_from, width - mask_from), s_ref[r, mask_from:width], NEG_INF)
        s_ref[r, mask_from:width] = tail
        m = jnp.max(tail, axis=-1, keepdims=True)
        if mask_from:
            m = jnp.maximum(m, jnp.max(s_ref[r, 0:mask_from], axis=-1, keepdims=True))
        r_ref[r, :] = lanes(m)
    for r in strips:
        e = jnp.exp(s_ref[r, 0:width] - wide(r_ref[r, :]))
        s_ref[r, 0:width] = e
        r_ref[r, :] = lanes(1.0 / jnp.sum(e, axis=-1, keepdims=True))
    for r in strips:
        p_ref[r, 0:width] = (s_ref[r, 0:width] * wide(r_ref[r, :])).astype(bf16)


def _select_kernel(q_ref, kc_ref, vc_ref, oc_ref, sb_ref, *, seq):
    tq = q_ref.shape[0]
    t0 = pl.program_id(2) * tq
    hg = HEADS_PER_GROUP
    q = q_ref[...]
    qs = jnp.concatenate([q[:, h * HEAD_DIM:(h + 1) * HEAD_DIM] for h in range(hg)], axis=0)
    tpos4 = t0 + lax.rem(lax.broadcasted_iota(i32, (hg * tq, 1), 0), tq)

    ncmp = kc_ref.shape[0]
    nblk = seq // SEL_LEN
    s_c = _dot_nt(qs, kc_ref[...])
    cstart = lax.broadcasted_iota(i32, (1, ncmp), 1) * CMP_STRIDE
    p_c = _masked_softmax(s_c, (cstart + (CMP_LEN - 1)) <= tpos4).astype(bf16)
    o_c = _dot(p_c, vc_ref[...])
    oc_ref[...] = jnp.concatenate([o_c[h * tq:(h + 1) * tq] for h in range(hg)], axis=1)

    cs = lax.broadcasted_iota(i32, (nblk, ncmp), 1) * CMP_STRIDE
    ss = lax.broadcasted_iota(i32, (nblk, ncmp), 0) * SEL_LEN
    overlap_t = jnp.where((cs < ss + SEL_LEN) & (cs + CMP_LEN > ss), 1.0, 0.0).astype(bf16)
    imp4 = _dot_nt(overlap_t, p_c)
    imp = imp4[:, 0:tq]
    for h in range(1, hg):
        imp = imp + imp4[:, h * tq:(h + 1) * tq]
    j = lax.broadcasted_iota(i32, (nblk, tq), 0)
    qblk = jnp.right_shift(t0 + lax.broadcasted_iota(i32, (nblk, tq), 1), 6)
    forced = (j == 0) | (j == qblk) | (j == qblk - 1)
    impf = jnp.where(forced, SEL_FORCE, jnp.where(j <= qblk, imp, -SEL_FORCE))
    beaten = jnp.zeros((nblk, tq), i32)
    for i in range(nblk):
        ci = impf[i:i + 1, :]
        beats = (ci > impf) | ((ci == impf) & (j > i))
        beaten = beaten + jnp.where(beats, 1, 0)
    keep = (beaten < min(N_SEL, nblk)) & (j <= qblk)
    sel_bias = jnp.where(keep, 0.0, NEG_INF)
    if nblk < HEAD_DIM:
        sel_bias = jnp.concatenate([sel_bias, jnp.zeros((HEAD_DIM - nblk, tq), f32)], axis=0)
    sb_ref[...] = sel_bias.T.astype(bf16)


def _select(qn, kc, vc, bsz, seq):
    n = qn.shape[0]
    gw = HEADS_PER_GROUP * HEAD_DIM
    nt = seq // TQ_SELECT
    nsub = seq // CMP_STRIDE
    qblk = pl.BlockSpec((TQ_SELECT, gw), lambda b, g, t: (b * nt + t, g))
    full = pl.BlockSpec((None, None, nsub, HEAD_DIM), lambda b, g, t: (b, g, 0, 0))
    return pl.pallas_call(
        functools.partial(_select_kernel, seq=seq),
        grid=(bsz, KV_GROUPS, nt),
        in_specs=[qblk, full, full],
        out_specs=[qblk, pl.BlockSpec((None, None, TQ_SELECT, HEAD_DIM), lambda b, g, t: (b, g, t, 0))],
        out_shape=[
            jax.ShapeDtypeStruct((n, NSA_WIDTH), f32),
            jax.ShapeDtypeStruct((bsz, KV_GROUPS, seq, HEAD_DIM), bf16),
        ],
        compiler_params=_cparams(40, 3),
        name="nsa_select",
    )(qn, kc, vc)


def _attn_kernel(q_ref, oc_ref, sb_ref, ks_ref, vs_ref, kw_ref, vw_ref, gate_ref, o_ref,
                 os_ref, s_ref, p_ref, r_ref, *, seq):
    t = pl.program_id(2)
    hg = HEADS_PER_GROUP
    q = q_ref[...]
    qs = jnp.concatenate([q[:, h * HEAD_DIM:(h + 1) * HEAD_DIM] for h in range(hg)], axis=0)
    t0 = t * TQ
    halves = [slice(0, hg // 2 * TQ), slice(hg // 2 * TQ, hg * TQ)]
    sel_bias = sb_ref[...]
    q_aug = jnp.concatenate(
        [jnp.concatenate([q[:, h * HEAD_DIM:(h + 1) * HEAD_DIM], sel_bias], axis=1) for h in range(hg)], axis=0)

    def causal_ok(tok0, col0, ncols):
        tpos_s = t0 + tok0 + lax.broadcasted_iota(i32, (STRIP, 1), 0)
        return (col0 + lax.broadcasted_iota(i32, (1, ncols), 1)) <= tpos_s

    n_chunks = (t0 + TQ + KEY_CHUNK - 1) // KEY_CHUNK
    for c in range(seq // KEY_CHUNK):

        @pl.when(n_chunks == c + 1)
        def _(width=(c + 1) * KEY_CHUNK):
            for rows in halves:
                s_ref[rows, 0:width] = _dot_nt(q_aug[rows], ks_ref[0:width, :])
            _softmax_strips(s_ref, p_ref, r_ref, width, width - KEY_CHUNK, causal_ok)
            for rows in halves:
                os_ref[rows, :] = _dot(p_ref[rows, 0:width], vs_ref[0:width, :])

    o_s = os_ref[...]

    wk = WINDOW + TQ
    start = pl.multiple_of(jnp.maximum(t0 - WINDOW, 0), TQ)

    def window_ok(tok0, col0, ncols):
        tpos_s = t0 + tok0 + lax.broadcasted_iota(i32, (STRIP, 1), 0)
        wpos = start + col0 + lax.broadcasted_iota(i32, (1, ncols), 1)
        return (wpos <= tpos_s) & (wpos > tpos_s - WINDOW)

    for rows in halves:
        s_ref[rows, 0:wk] = _dot_nt(qs[rows], kw_ref[pl.ds(start, wk), :])
    _softmax_strips(s_ref, p_ref, r_ref, wk, 0, window_ok)
    o_w = jnp.concatenate([_dot(p_ref[rows, 0:wk], vw_ref[pl.ds(start, wk), :]) for rows in halves], axis=0)

    gts = gate_ref[...]
    o_c = oc_ref[...]
    outs = []
    for h in range(hg):
        rows = slice(h * TQ, (h + 1) * TQ)
        outs.append(gts[:, 3 * h:3 * h + 1] * o_c[:, h * HEAD_DIM:(h + 1) * HEAD_DIM]
                    + gts[:, 3 * h + 1:3 * h + 2] * o_s[rows] + gts[:, 3 * h + 2:3 * h + 3] * o_w[rows])
    o_ref[...] = jnp.concatenate(outs, axis=1)


def _attention(qn, o_cmp, sel_bias, ks, vs, kw, vw, gates, bsz, seq):
    n = qn.shape[0]
    gw = HEADS_PER_GROUP * HEAD_DIM
    ntq = seq // TQ
    ng = 3 * HEADS_PER_GROUP
    qblk = pl.BlockSpec((TQ, gw), lambda b, g, t: (b * ntq + t, g))
    full = lambda r, w: pl.BlockSpec((None, None, r, w), lambda b, g, t: (b, g, 0, 0))
    return pl.pallas_call(
        functools.partial(_attn_kernel, seq=seq),
        grid=(bsz, KV_GROUPS, ntq),
        in_specs=[
            qblk, qblk, pl.BlockSpec((None, None, TQ, HEAD_DIM), lambda b, g, t: (b, g, t, 0)),
            full(seq, 2 * HEAD_DIM), full(seq, HEAD_DIM), full(seq, HEAD_DIM), full(seq, HEAD_DIM),
            pl.BlockSpec((None, None, TQ, ng), lambda b, g, t: (b, g, t, 0)),
        ],
        out_specs=qblk,
        out_shape=jax.ShapeDtypeStruct((n, NSA_WIDTH), f32),
        scratch_shapes=[
            pltpu.VMEM((HEADS_PER_GROUP * TQ, HEAD_DIM), f32),
            pltpu.VMEM((HEADS_PER_GROUP * TQ, seq), f32),
            pltpu.VMEM((HEADS_PER_GROUP * TQ, seq), bf16),
            pltpu.VMEM((HEADS_PER_GROUP * TQ, LANES), f32),
        ],
        compiler_params=_cparams(48, 3),
        name="nsa_attention",
    )(qn, o_cmp, sel_bias, ks, vs, kw, vw, gates)


def _lru_kernel(xr_ref, xg_ref, cw_ref, cb_ref, wr_ref, br_ref, wi_ref, bi_ref, lam_ref, o_ref,
                xbuf, hstate, abuf, ubuf):
    ts = xr_ref.shape[0]
    s = pl.program_id(1)

    @pl.when(s == 0)
    def _():
        xbuf[0:8, :] = jnp.zeros((8, LRU_WIDTH), f32)
        hstate[...] = jnp.zeros_like(hstate)

    xbuf[8:8 + ts, :] = xr_ref[...]
    acc = xbuf[5:5 + ts, :] * cw_ref[0:1, :]
    for k in range(1, CONV_W):
        acc = acc + xbuf[5 + k:5 + k + ts, :] * cw_ref[k:k + 1, :]
    xc = cb_ref[...] + acc
    xbuf[0:8, :] = xbuf[ts:ts + 8, :]

    xcb = xc.astype(bf16)
    r = jax.nn.sigmoid(_dot(xcb, wr_ref[...]) + br_ref[...])
    ig = jax.nn.sigmoid(_dot(xcb, wi_ref[...]) + bi_ref[...])
    nl = -lam_ref[...]
    softplus = jnp.maximum(nl, 0.0) + jnp.log1p(jnp.exp(-jnp.abs(nl)))
    log_a = (-RG_C * softplus) * r
    abuf[...] = jnp.exp(log_a)
    th = jnp.tanh(log_a)
    ubuf[...] = jnp.sqrt(-2.0 * th / (1.0 - th)) * (ig * xc)

    def step(jb, h):
        base = pl.multiple_of(jb * 8, 8)
        a8 = abuf[pl.ds(base, 8), :]
        u8 = ubuf[pl.ds(base, 8), :]
        rows = []
        for rr in range(8):
            h = a8[rr:rr + 1, :] * h + u8[rr:rr + 1, :]
            rows.append(h)
        ubuf[pl.ds(base, 8), :] = jnp.concatenate(rows, axis=0)
        return h

    hstate[0:1, :] = lax.fori_loop(0, ts // 8, step, hstate[0:1, :])
    xg = xg_ref[...]
    gelu = 0.5 * xg * (1.0 + jnp.tanh(0.7978845608028654 * (xg + 0.044715 * (xg * xg * xg))))
    o_ref[...] = ubuf[...] * gelu


def _block_diag(w):
    nb, bw, _ = w.shape
    n = nb * bw
    tiled = jnp.tile(w.reshape(n, bw), (1, nb))
    same = (lax.broadcasted_iota(i32, (n, n), 0) // bw) == (lax.broadcasted_iota(i32, (n, n), 1) // bw)
    return jnp.where(same, tiled, 0.0)


def _lru(zx, conv_w, conv_b, w_rg, b_rg, w_ig, b_ig, lam, bsz, seq):
    n = zx.shape[0]
    ts = TS_LRU
    tpb = seq // ts
    w = LRU_WIDTH
    row = lambda v: v.reshape(1, w)
    cst = lambda r, c: pl.BlockSpec((r, c), lambda b, s: (0, 0))
    return pl.pallas_call(
        _lru_kernel,
        grid=(bsz, tpb),
        in_specs=[
            pl.BlockSpec((ts, w), lambda b, s: (b * tpb + s, 0)),
            pl.BlockSpec((ts, w), lambda b, s: (b * tpb + s, 1)),
            cst(CONV_W, w), cst(1, w), cst(w, w), cst(1, w), cst(w, w), cst(1, w), cst(1, w),
        ],
        out_specs=pl.BlockSpec((ts, w), lambda b, s: (b * tpb + s, 0)),
        out_shape=jax.ShapeDtypeStruct((n, w), f32),
        scratch_shapes=[
            pltpu.VMEM((ts + 8, w), f32), pltpu.VMEM((8, w), f32),
            pltpu.VMEM((ts, w), f32), pltpu.VMEM((ts, w), f32),
        ],
        compiler_params=_cparams(40, 2),
        name="rglru",
    )(zx, zx, conv_w, row(conv_b), _block_diag(w_rg).astype(bf16), row(b_rg),
      _block_diag(w_ig).astype(bf16), row(b_ig), row(lam))


def _outproj_kernel(on_ref, ol_ref, x_ref, mod_ref, gn_ref, gl_ref, w_ref, g2_ref, wrh_ref, wrl_ref, br_ref,
                    x1_ref, hp_ref, te_ref, tg_ref):
    n_rows = x_ref.shape[0]
    for rows in (slice(0, n_rows // 2), slice(n_rows // 2, n_rows)):
        _outproj_rows(rows, on_ref, ol_ref, x_ref, mod_ref, gn_ref, gl_ref, w_ref, g2_ref, wrh_ref, wrl_ref, br_ref,
                      x1_ref, hp_ref, te_ref, tg_ref)


def _outproj_rows(rows, on_ref, ol_ref, x_ref, mod_ref, gn_ref, gl_ref, w_ref, g2_ref, wrh_ref, wrl_ref, br_ref,
                  x1_ref, hp_ref, te_ref, tg_ref):
    nn = _rms_rows(on_ref[rows, :], gn_ref[...])
    nl = _rms_rows(ol_ref[rows, :], gl_ref[...])
    hcat = jnp.concatenate([nn, nl], axis=1).astype(bf16)
    mix = _dot(hcat, w_ref[...])
    x1 = x_ref[rows, :] + mod_ref[2:3, :] * mix
    x1_ref[rows, :] = x1
    h2 = _rms_rows(x1, g2_ref[...]) * (1.0 + mod_ref[4:5, :]) + mod_ref[3:4, :]

    half = D_MODEL // 2
    hb = h2.astype(bf16).astype(f32)
    hi_bits = lax.bitcast_convert_type(hb[:, :half], i32)
    lo_bits = lax.shift_right_logical(lax.bitcast_convert_type(hb[:, half:], i32), 16)
    hp_ref[rows, :] = hi_bits | lo_bits

    hh, hl = _split_bf16(h2)
    logits = _dot(hh, wrh_ref[...]) + _dot(hl, wrh_ref[...]) + _dot(hh, wrl_ref[...]) + br_ref[...]
    lane = lax.broadcasted_iota(i32, (1, LANES), 1)
    lane_f = lane.astype(f32)
    vals, idxs = [], []
    cur = logits
    for _ in range(TOP_K):
        m = jnp.max(cur, axis=-1, keepdims=True)
        idx = jnp.min(jnp.where(cur == m, lane_f, float(LANES)), axis=-1, keepdims=True).astype(i32)
        vals.append(m)
        idxs.append(idx)
        cur = jnp.where(lane == idx, -3e38, cur)
    es = [jnp.exp(v - vals[0]) for v in vals]
    den = es[0]
    for e in es[1:]:
        den = den + e
    inv = 1.0 / den
    te = jnp.full(logits.shape, -1, i32)
    tg = jnp.zeros(logits.shape, f32)
    for k in range(TOP_K):
        te = jnp.where(lane == k, idxs[k], te)
        tg = jnp.where(lane == k, es[k] * inv, tg)
    te_ref[rows, :] = te
    tg_ref[rows, :] = tg


def _outproj(o_nsa, o_lru, xf, mod3, g_out_nsa, g_out_lru, w_out, g_norm2, w_router, b_router, seq):
    n, d = xf.shape
    tm = TM_OUT
    tpb = seq // tm
    wr = jnp.zeros((d, LANES), f32).at[:, :N_EXPERTS].set(w_router)
    wrh = wr.astype(bf16)
    wrl = (wr - wrh.astype(f32)).astype(bf16)
    br = jnp.full((1, LANES), NEG_INF, f32).at[0, :N_EXPERTS].set(b_router)
    rows = lambda w: pl.BlockSpec((tm, w), lambda i: (i, 0))
    cst = lambda r, c: pl.BlockSpec((r, c), lambda i: (0, 0))
    return pl.pallas_call(
        _outproj_kernel,
        grid=(n // tm,),
        in_specs=[
            rows(NSA_WIDTH), rows(LRU_WIDTH), rows(d),
            pl.BlockSpec((None, 6, d), lambda i: (i // tpb, 0, 0)),
            cst(1, NSA_WIDTH), cst(1, LRU_WIDTH), cst(d, d), cst(1, d), cst(d, LANES), cst(d, LANES), cst(1, LANES),
        ],
        out_specs=[rows(d), rows(d // 2), rows(LANES), rows(LANES)],
        out_shape=[
            jax.ShapeDtypeStruct((n, d), f32), jax.ShapeDtypeStruct((n, d // 2), i32),
            jax.ShapeDtypeStruct((n, LANES), i32), jax.ShapeDtypeStruct((n, LANES), f32),
        ],
        compiler_params=_cparams(56, 1),
        name="out_proj_router",
    )(o_nsa, o_lru, xf, mod3, g_out_nsa.reshape(1, -1), g_out_lru.reshape(1, -1), w_out.astype(bf16),
      g_norm2.reshape(1, d), wrh, wrl, br)


def _rank_kernel(te_ref, rank_ref, cnt_ref, carry):
    i = pl.program_id(0)
    tm = te_ref.shape[0]

    @pl.when(i == 0)
    def _():
        carry[...] = jnp.zeros_like(carry)

    te = te_ref[...]
    lane = lax.broadcasted_iota(i32, (1, LANES), 1)
    hits = [te[:, k:k + 1] == lane for k in range(TOP_K)]
    onehot = jnp.zeros((tm, LANES), f32)
    for h in hits:
        onehot = onehot + jnp.where(h, 1.0, 0.0)
    r = lax.broadcasted_iota(i32, (tm, tm), 0)
    c = lax.broadcasted_iota(i32, (tm, tm), 1)
    lower = jnp.where(c < r, 1.0, 0.0).astype(bf16)
    prefix = _dot(lower, onehot.astype(bf16)) + carry[0:1, :]
    rank = jnp.zeros((tm, LANES), f32)
    for k in range(TOP_K):
        rk = jnp.sum(jnp.where(hits[k], prefix, 0.0), axis=-1, keepdims=True)
        rank = jnp.where(lane == k, rk, rank)
    rank_ref[...] = rank.astype(i32)
    carry[0:1, :] = carry[0:1, :] + jnp.sum(onehot, axis=0, keepdims=True)
    cnt_ref[...] = carry[...]


def _ranks(te_pad):
    n = te_pad.shape[0]
    tm = TM_ROUTE
    return pl.pallas_call(
        _rank_kernel,
        grid=(n // tm,),
        in_specs=[pl.BlockSpec((tm, LANES), lambda i: (i, 0))],
        out_specs=[pl.BlockSpec((tm, LANES), lambda i: (i, 0)), pl.BlockSpec((8, LANES), lambda i: (0, 0))],
        out_shape=[jax.ShapeDtypeStruct((n, LANES), i32), jax.ShapeDtypeStruct((8, LANES), f32)],
        scratch_shapes=[pltpu.VMEM((8, LANES), f32)],
        compiler_params=_cparams(32, 1),
        name="route_ranks",
    )(te_pad)


def _slot_kernel(te_ref, rank_ref, pstart_ref, dest_ref):
    te = te_ref[...]
    lane = lax.broadcasted_iota(i32, (1, LANES), 1)
    ps = pstart_ref[...].astype(f32)
    dest = jnp.zeros(te.shape, i32)
    for k in range(TOP_K):
        base = jnp.sum(jnp.where(te[:, k:k + 1] == lane, ps, 0.0), axis=-1, keepdims=True)
        dest = jnp.where(lane == k, base.astype(i32), dest)
    dest_ref[...] = dest + rank_ref[...]


def _slots(te_pad, rank_pad, pstart_row):
    n = te_pad.shape[0]
    tm = TM_ROUTE
    blk = pl.BlockSpec((tm, LANES), lambda i: (i, 0))
    return pl.pallas_call(
        _slot_kernel,
        grid=(n // tm,),
        in_specs=[blk, blk, pl.BlockSpec((1, LANES), lambda i: (0, 0))],
        out_specs=blk,
        out_shape=jax.ShapeDtypeStruct((n, LANES), i32),
        compiler_params=_cparams(32, 1),
        name="route_slots",
    )(te_pad, rank_pad, pstart_row)


def _row_copy(src, src_row, dst, dst_row, sem):
    return pltpu.make_async_copy(src.at[pl.ds(src_row, 1)], dst.at[pl.ds(dst_row, 1)], sem)


def _dispatch_kernel(dest_ref, fill_ref, h_ref, xs_ref, zbuf, sem, zsem):
    tm = h_ref.shape[0]

    @pl.when(pl.program_id(0) == 0)
    def _():
        zbuf[...] = jnp.zeros_like(zbuf)

        def fill(e, c):
            start = pl.multiple_of(fill_ref[e], 8)
            cp = pltpu.make_async_copy(zbuf, xs_ref.at[pl.ds(start, FILL_ROWS)], zsem)
            cp.start()
            cp.wait()
            return c

        lax.fori_loop(0, fill_ref.shape[0], fill, 0)

    def issue(i, c):
        for k in range(TOP_K):
            _row_copy(h_ref, i, xs_ref, dest_ref[0, i * TOP_K + k], sem).start(priority=k % 2)
        return c

    lax.fori_loop(0, tm, issue, 0)

    def drain(i, c):
        for k in range(TOP_K):
            _row_copy(h_ref, 0, xs_ref, 0, sem).wait()
        return c

    lax.fori_loop(0, tm, drain, 0)


def _dispatch(dest_tiles, fill_start, hp, n_slots):
    n, w = hp.shape
    tm = TM_DISP
    return pl.pallas_call(
        _dispatch_kernel,
        grid=(n // tm,),
        in_specs=[
            pl.BlockSpec((None, 1, tm * TOP_K), lambda i: (i, 0, 0), memory_space=pltpu.SMEM),
            pl.BlockSpec(memory_space=pltpu.SMEM),
            pl.BlockSpec((tm, w), lambda i: (i, 0)),
        ],
        out_specs=pl.BlockSpec(memory_space=pl.ANY),
        out_shape=jax.ShapeDtypeStruct((n_slots, w), i32),
        scratch_shapes=[pltpu.VMEM((FILL_ROWS, w), i32), pltpu.SemaphoreType.DMA(()), pltpu.SemaphoreType.DMA(())],
        compiler_params=_cparams(32, 1),
        name="moe_dispatch",
    )(dest_tiles, fill_start, hp)


def _unpack_rows(xp):
    hi = lax.bitcast_convert_type(xp & jnp.int32(-65536), f32)
    lo = lax.bitcast_convert_type(lax.shift_left(xp, 16), f32)
    return jnp.concatenate([hi, lo], axis=1).astype(bf16)


def _stage_weights(i, n_live, be_ref, nxt_ref, slot_ref, copies, cast):
    first = (i < n_live) & ((i == 0) | (be_ref[i] != be_ref[jnp.maximum(i - 1, 0)]))
    slot = slot_ref[i]

    @pl.when(first & (i == 0))
    def _():
        for cp in copies(be_ref[i], slot):
            cp.start()

    @pl.when(first)
    def _():
        for cp in copies(be_ref[i], slot):
            cp.wait()

        @pl.when(nxt_ref[i] >= 0)
        def _():
            for cp in copies(nxt_ref[i], 1 - slot):
                cp.start()

        cast(slot)


def _up_kernel(be_ref, nb_ref, nxt_ref, slot_ref, xs_ref, w_hbm, bg_ref, bl_ref, act_ref, stage, wgb, wlb, sem):
    n = pl.program_id(0)
    i = pl.program_id(1)
    live = i < nb_ref[0]
    tf = wgb.shape[1]

    def copies(e, s):
        cols = [pl.ds(pl.multiple_of(half * D_FF + n * tf, LANES), tf) for half in range(2)]
        return [pltpu.make_async_copy(w_hbm.at[e, :, cols[half]], stage.at[s, half], sem.at[s]) for half in range(2)]

    def cast(s):
        wgb[...] = stage[s, 0].astype(bf16)
        wlb[...] = stage[s, 1].astype(bf16)

    _stage_weights(i, nb_ref[0], be_ref, nxt_ref, slot_ref, copies, cast)

    @pl.when(live)
    def _():
        x = _unpack_rows(xs_ref[...])
        ug = _dot(x, wgb[...]) + bg_ref[...]
        ul = _dot(x, wlb[...]) + bl_ref[...]
        ug = jnp.minimum(ug, SWIGLU_LIMIT)
        ul = jnp.clip(ul, -SWIGLU_LIMIT, SWIGLU_LIMIT)
        act_ref[...] = (ug * jax.nn.sigmoid(SWIGLU_ALPHA * ug) * (ul + 1.0)).astype(bf16)

    @pl.when(pl.program_id(1) >= nb_ref[0])
    def _():
        act_ref[...] = jnp.zeros_like(act_ref)


def _expert_up(route, xs, w_e1, b_e1):
    n_slots, w = xs.shape
    nblk = n_slots // MOE_BLK
    tf = TF_UP
    ncol = D_FF // tf
    d = D_MODEL
    grid_spec = pltpu.PrefetchScalarGridSpec(
        num_scalar_prefetch=4,
        grid=(ncol, nblk),
        in_specs=[
            pl.BlockSpec((MOE_BLK, w), lambda n, i, be, nb, nx, sl: (jnp.minimum(i, nb[0] - 1), 0)),
            pl.BlockSpec(memory_space=pl.ANY),
            pl.BlockSpec((None, 1, tf), lambda n, i, be, nb, nx, sl: (be[i], 0, n)),
            pl.BlockSpec((None, 1, tf), lambda n, i, be, nb, nx, sl: (be[i], 0, ncol + n)),
        ],
        out_specs=pl.BlockSpec((MOE_BLK, tf), lambda n, i, be, nb, nx, sl: (i, n)),
        scratch_shapes=[
            pltpu.VMEM((2, 2, d, tf), f32), pltpu.VMEM((d, tf), bf16), pltpu.VMEM((d, tf), bf16),
            pltpu.SemaphoreType.DMA((2,)),
        ],
    )
    return pl.pallas_call(
        _up_kernel,
        grid_spec=grid_spec,
        out_shape=jax.ShapeDtypeStruct((n_slots, D_FF), bf16),
        compiler_params=_cparams(56, 2),
        name="moe_up",
    )(*route, xs, w_e1, b_e1.reshape(N_EXPERTS, 1, 2 * D_FF), b_e1.reshape(N_EXPERTS, 1, 2 * D_FF))


def _down_kernel(be_ref, nb_ref, nxt_ref, slot_ref, act_ref, w_hbm, b_ref, y_ref, stage, wb, sem):
    i = pl.program_id(1)
    live = i < nb_ref[0]

    def copies(e, s):
        return [pltpu.make_async_copy(w_hbm.at[e], stage.at[s], sem.at[s])]

    def cast(s):
        wb[...] = stage[s].astype(bf16)

    _stage_weights(i, nb_ref[0], be_ref, nxt_ref, slot_ref, copies, cast)

    @pl.when(live)
    def _():
        y_ref[...] = _dot(act_ref[...], wb[...]) + b_ref[...]

    @pl.when(pl.program_id(1) >= nb_ref[0])
    def _():
        y_ref[...] = jnp.zeros_like(y_ref)


def _expert_down(route, act, w_e2, b_e2):
    n_slots = act.shape[0]
    nblk = n_slots // MOE_BLK
    grid_spec = pltpu.PrefetchScalarGridSpec(
        num_scalar_prefetch=4,
        grid=(1, nblk),
        in_specs=[
            pl.BlockSpec((MOE_BLK, D_FF), lambda n, i, be, nb, nx, sl: (i, 0)),
            pl.BlockSpec(memory_space=pl.ANY),
            pl.BlockSpec((None, 1, D_MODEL), lambda n, i, be, nb, nx, sl: (be[i], 0, 0)),
        ],
        out_specs=pl.BlockSpec((MOE_BLK, D_MODEL), lambda n, i, be, nb, nx, sl: (i, 0)),
        scratch_shapes=[
            pltpu.VMEM((2, D_FF, D_MODEL), f32), pltpu.VMEM((D_FF, D_MODEL), bf16), pltpu.SemaphoreType.DMA((2,)),
        ],
    )
    return pl.pallas_call(
        _down_kernel,
        grid_spec=grid_spec,
        out_shape=jax.ShapeDtypeStruct((n_slots, D_MODEL), f32),
        compiler_params=_cparams(56, 2),
        name="moe_down",
    )(*route, act, w_e2, b_e2.reshape(N_EXPERTS, 1, D_MODEL))


def _combine_kernel(dest_ref, dest_next_ref, y_ref, x1_ref, tg_ref, mod_ref, o_ref, buf, sem):
    tm = x1_ref.shape[0]
    step = pl.program_id(0)
    slot = step % 2

    def gather(d_ref, s):
        def issue(i, c):
            for k in range(TOP_K):
                _row_copy(y_ref, d_ref[0, i * TOP_K + k], buf.at[s, k], i, sem.at[s]).start(priority=k % 2)
            return c

        lax.fori_loop(0, tm, issue, 0)

    @pl.when(step == 0)
    def _():
        gather(dest_ref, 0)

    @pl.when(step + 1 < pl.num_programs(0))
    def _():
        gather(dest_next_ref, 1 - slot)

    def drain(i, c):
        for k in range(TOP_K):
            _row_copy(y_ref, 0, buf.at[slot, k], 0, sem.at[slot]).wait()
        return c

    lax.fori_loop(0, tm, drain, 0)
    tg = tg_ref[...]
    acc = tg[:, 0:1] * buf[slot, 0]
    for k in range(1, TOP_K):
        acc = acc + tg[:, k:k + 1] * buf[slot, k]
    o_ref[...] = x1_ref[...] + mod_ref[5:6, :] * acc


def _combine(dest_tiles, y, x1, tg_pad, mod3, seq):
    n, d = x1.shape
    tm = TM_COMB
    tpb = seq // tm
    return pl.pallas_call(
        _combine_kernel,
        grid=(n // tm,),
        in_specs=[
            pl.BlockSpec((None, 1, tm * TOP_K), lambda i: (i, 0, 0), memory_space=pltpu.SMEM),
            pl.BlockSpec((None, 1, tm * TOP_K), lambda i: (jnp.minimum(i + 1, n // tm - 1), 0, 0),
                         memory_space=pltpu.SMEM),
            pl.BlockSpec(memory_space=pl.ANY),
            pl.BlockSpec((tm, d), lambda i: (i, 0)),
            pl.BlockSpec((tm, LANES), lambda i: (i, 0)),
            pl.BlockSpec((None, 6, d), lambda i: (i // tpb, 0, 0)),
        ],
        out_specs=pl.BlockSpec((tm, d), lambda i: (i, 0)),
        out_shape=jax.ShapeDtypeStruct((n, d), f32),
        scratch_shapes=[pltpu.VMEM((2, TOP_K, tm, d), f32), pltpu.SemaphoreType.DMA((2,))],
        compiler_params=_cparams(32, 1),
        name="moe_combine",
    )(dest_tiles, dest_tiles, y, x1, tg_pad, mod3)


def _layer(x, mod, g_norm1, w_in, pe_cmp_k, pe_cmp_v, w_cmp_k, w_cmp_v, q_gain, k_gain, conv_w, conv_b,
           w_rg, b_rg, w_ig, b_ig, lru_lambda, g_out_nsa, g_out_lru, w_out, g_norm2, w_router, b_router,
           w_e1, b_e1, w_e2, b_e2):
    bsz, seq, d = x.shape
    n = bsz * seq
    xf = x.reshape(n, d)
    mod3 = mod.reshape(bsz, 6, d)

    gate_col = NSA_WIDTH + 6 * KV_WIDTH
    n_gate = 3 * N_HEADS
    w_pad = jnp.concatenate(
        [w_in[:, :gate_col + n_gate], jnp.zeros((d, GATE_PAD - n_gate), w_in.dtype), w_in[:, gate_col + n_gate:]],
        axis=1).astype(bf16)
    qn, ks, vs, kw, vw, gates, zc, zx = _inproj(xf, g_norm1, mod3, w_pad, q_gain, k_gain, bsz, seq)
    kc, vc = _compress(zc, pe_cmp_k, pe_cmp_v, w_cmp_k, w_cmp_v, k_gain[0], bsz, seq)
    o_cmp, sel_bias = _select(qn, kc, vc, bsz, seq)
    o_nsa = _attention(qn, o_cmp, sel_bias, ks, vs, kw, vw, gates, bsz, seq)
    o_lru = _lru(zx, conv_w, conv_b, w_rg, b_rg, w_ig, b_ig, lru_lambda, bsz, seq)

    x1, hp, te_pad, tg_pad = _outproj(o_nsa, o_lru, xf, mod3, g_out_nsa, g_out_lru, w_out, g_norm2,
                                      w_router, b_router, seq)

    rank_pad, cnt = _ranks(te_pad)
    counts = cnt[0, :N_EXPERTS].astype(i32)
    pcounts = (counts + MOE_BLK - 1) // MOE_BLK * MOE_BLK
    pends = jnp.cumsum(pcounts)
    pstarts = pends - pcounts
    n_blocks = (n * TOP_K + N_EXPERTS * (MOE_BLK - 1) + MOE_BLK - 1) // MOE_BLK
    n_slots = n_blocks * MOE_BLK
    blk_start = jnp.arange(n_blocks, dtype=i32) * MOE_BLK
    blk_e = jnp.minimum(jnp.sum((pends[None, :] <= blk_start[:, None]).astype(i32), axis=1), N_EXPERTS - 1)
    n_used = (pends[-1] // MOE_BLK).astype(i32).reshape(1)
    n_tail = (n_slots - n * TOP_K + MOE_BLK - 1) // MOE_BLK
    fill_rows = jnp.concatenate([pstarts + counts, pends[-1] + jnp.arange(n_tail, dtype=i32) * MOE_BLK])
    fill_start = jnp.minimum(fill_rows // 8 * 8, n_slots - FILL_ROWS).astype(i32)
    pstart_row = jnp.zeros((1, LANES), i32).at[0, :N_EXPERTS].set(pstarts.astype(i32))
    dest_pad = _slots(te_pad, rank_pad, pstart_row)
    dest = dest_pad[:, :TOP_K]

    xs = _dispatch(dest.reshape(n // TM_DISP, 1, TM_DISP * TOP_K), fill_start, hp, n_slots)
    run_first = jnp.concatenate([jnp.ones((1,), bool), blk_e[1:] != blk_e[:-1]])
    slot = ((jnp.cumsum(run_first.astype(i32)) - 1) % 2).astype(i32)
    later = lax.cummin(jnp.where(counts > 0, jnp.arange(N_EXPERTS, dtype=i32), N_EXPERTS), reverse=True)
    nxt_e = jnp.concatenate([later[1:], jnp.full((1,), N_EXPERTS, i32)])
    nxt = jnp.where(nxt_e < N_EXPERTS, nxt_e, -1)[blk_e].astype(i32)
    route = (blk_e, n_used, nxt, slot)
    act = _expert_up(route, xs, w_e1, b_e1)
    y = _expert_down(route, act, w_e2, b_e2)
    out = _combine(dest.reshape(n // TM_COMB, 1, TM_COMB * TOP_K), y, x1, tg_pad, mod3, seq)
    return out.reshape(bsz, seq, d)


def kernel(x, c, w_ada, b_ada, g_norm1, w_in, pe_cmp_k, pe_cmp_v, w_cmp_k, w_cmp_v, q_gain, k_gain, conv_w, conv_b, w_rg, b_rg, w_ig, b_ig, lru_lambda, g_out_nsa, g_out_lru, w_out, g_norm2, w_router, b_router, w_e1, b_e1, w_e2, b_e2):
    for l in range(w_ada.shape[0]):
        mod = _ada_mod(c, w_ada[l], b_ada[l])
        x = _layer(x, mod, g_norm1[l], w_in[l], pe_cmp_k[l], pe_cmp_v[l], w_cmp_k[l], w_cmp_v[l], q_gain[l],
                   k_gain[l], conv_w[l], conv_b[l], w_rg[l], b_rg[l], w_ig[l], b_ig[l], lru_lambda[l],
                   g_out_nsa[l], g_out_lru[l], w_out[l], g_norm2[l], w_router[l], b_router[l], w_e1[l], b_e1[l],
                   w_e2[l], b_e2[l])
    return x
```

```python
import functools

import jax
import jax.numpy as jnp
from jax import lax
from jax.experimental import pallas as pl
from jax.experimental.pallas import tpu as pltpu

f32 = jnp.float32
bf16 = jnp.bfloat16
i32 = jnp.int32

D_MODEL = 2048
N_HEADS = 16
HEAD_DIM = 64
KV_GROUPS = 4
HEADS_PER_GROUP = N_HEADS // KV_GROUPS
NSA_WIDTH = N_HEADS * HEAD_DIM
KV_WIDTH = KV_GROUPS * HEAD_DIM
CMP_LEN = 32
CMP_STRIDE = 16
SEL_LEN = 64
N_SEL = 8
WINDOW = 512
LRU_WIDTH = D_MODEL - NSA_WIDTH
LRU_BLOCKS = 16
CONV_W = 4
RG_C = 8.0
N_EXPERTS = 32
TOP_K = 4
D_FF = D_MODEL
SWIGLU_LIMIT = 7.0
SWIGLU_ALPHA = 1.702
NORM_EPS = 1e-6
NEG_INF = -1e30
SEL_FORCE = 1e30
GATE_PAD = 128
IN_PAD = NSA_WIDTH + 6 * KV_WIDTH + GATE_PAD + 2 * LRU_WIDTH
LANES = 128
MIB = 1024 * 1024

TM_PROJ = 256
TM_OUT = 512
TQ = 256
TQ_SELECT = 1024
KEY_CHUNK = 256
STRIP = 16
TS_LRU = 256
TM_ROUTE = 512
MOE_BLK = 256
FILL_ROWS = MOE_BLK + 8
TM_DISP = 512
TM_COMB = 256
TF_UP = 1024


def _cparams(vmem_mib, n_axes):
    return pltpu.CompilerParams(
        vmem_limit_bytes=int(vmem_mib * MIB),
        dimension_semantics=("arbitrary",) * n_axes,
    )


def _dot(a, b):
    return jnp.dot(a, b, preferred_element_type=f32)


def _dot_nt(a, b):
    return lax.dot_general(a, b, (((1,), (1,)), ((), ())), preferred_element_type=f32)


def _split_bf16(x):
    hi = x.astype(bf16)
    lo = (x - hi.astype(f32)).astype(bf16)
    return hi, lo


def _group_meansq(x, group):
    w = x.shape[1]
    r = lax.broadcasted_iota(i32, (w, w), 0) // group
    c = lax.broadcasted_iota(i32, (w, w), 1) // group
    ones_bd = jnp.where(r == c, 1.0, 0.0).astype(bf16)
    hi, lo = _split_bf16(x * x)
    return (_dot(hi, ones_bd) + _dot(lo, ones_bd)) * (1.0 / group)


def _rms_rows(x, gain):
    ms = jnp.mean(x * x, axis=-1, keepdims=True)
    return x * lax.rsqrt(ms + NORM_EPS) * gain


def _masked_softmax(s, m):
    sm = jnp.where(m, s, NEG_INF)
    mx = jnp.max(sm, axis=-1, keepdims=True)
    e = jnp.where(m, jnp.exp(sm - mx), 0.0)
    den = jnp.sum(e, axis=-1, keepdims=True)
    inv = jnp.where(den > 0.0, 1.0 / den, 0.0)
    return e * inv


def _bias_softmax(sb):
    mx = jnp.max(sb, axis=-1, keepdims=True)
    e = jnp.exp(sb - mx)
    den = jnp.sum(e, axis=-1, keepdims=True)
    return (e * (1.0 / den)).astype(bf16)


def _ada_kernel(c_ref, w_ref, b_ref, o_ref):
    c = c_ref[...]
    sc = c * jax.nn.sigmoid(c)
    o_ref[...] = _dot(sc.astype(bf16), w_ref[...].astype(bf16)) + b_ref[...]


def _ada_mod(c, w_ada, b_ada):
    bsz, d = c.shape
    n = w_ada.shape[1]
    tn = 1024
    return pl.pallas_call(
        _ada_kernel,
        grid=(n // tn,),
        in_specs=[
            pl.BlockSpec((bsz, d), lambda j: (0, 0)),
            pl.BlockSpec((d, tn), lambda j: (0, j)),
            pl.BlockSpec((1, tn), lambda j: (0, j)),
        ],
        out_specs=pl.BlockSpec((bsz, tn), lambda j: (0, j)),
        out_shape=jax.ShapeDtypeStruct((bsz, n), f32),
        compiler_params=_cparams(40, 1),
        name="ada_mod",
    )(c, w_ada, b_ada.reshape(1, n))


def _inproj_kernel(x_ref, g_ref, mod_ref, w_ref, qg_ref, kg_ref,
                   qn_ref, ks_ref, vs_ref, kw_ref, vw_ref, gate_ref, zc_ref, zx_ref, *, tiles_per_seq):
    x = x_ref[...]
    y = _rms_rows(x, g_ref[...])
    h = y * (1.0 + mod_ref[1:2, :]) + mod_ref[0:1, :]
    z = _dot(h.astype(bf16), w_ref[...])
    kv0 = NSA_WIDTH
    zc_ref[...] = z[:, kv0:kv0 + 2 * KV_WIDTH]
    zx_ref[...] = z[:, IN_PAD - 2 * LRU_WIDTH:]

    gw = HEADS_PER_GROUP * HEAD_DIM
    qg = qg_ref[...]
    for g in range(KV_GROUPS):
        xg = z[:, g * gw:(g + 1) * gw]
        ms = _group_meansq(xg, HEAD_DIM)
        qn = xg * lax.rsqrt(ms + NORM_EPS) * qg * (HEAD_DIM ** -0.5)
        qn_ref[:, g * gw:(g + 1) * gw] = qn.astype(bf16)

    def kv_part(col):
        return z[:, kv0 + col * KV_WIDTH:kv0 + (col + 1) * KV_WIDTH]

    def norm_k(col, row):
        xk = kv_part(col)
        ms = _group_meansq(xk, HEAD_DIM)
        return xk * lax.rsqrt(ms + NORM_EPS) * kg_ref[row:row + 1, :]

    ksn = norm_k(2, 1)
    kwn = norm_k(4, 2)
    vs = kv_part(3)
    vw = kv_part(5)
    gl = jax.nn.sigmoid(z[:, kv0 + 6 * KV_WIDTH:kv0 + 6 * KV_WIDTH + GATE_PAD])
    ng = 3 * HEADS_PER_GROUP
    tm = x_ref.shape[0]
    pos = (pl.program_id(0) % tiles_per_seq) * tm + lax.broadcasted_iota(i32, (tm, HEAD_DIM), 0)
    blk_onehot = jnp.where(jnp.right_shift(pos, 6) == lax.broadcasted_iota(i32, (tm, HEAD_DIM), 1), 1.0, 0.0)
    for g in range(KV_GROUPS):
        sl = slice(g * HEAD_DIM, (g + 1) * HEAD_DIM)
        ks_ref[g] = jnp.concatenate([ksn[:, sl], blk_onehot], axis=1).astype(bf16)
        vs_ref[g] = vs[:, sl].astype(bf16)
        kw_ref[g] = kwn[:, sl].astype(bf16)
        vw_ref[g] = vw[:, sl].astype(bf16)
        gate_ref[g] = gl[:, g * ng:(g + 1) * ng]


def _inproj(xf, g1, mod3, w_pad, q_gain, k_gain, bsz, seq):
    n, d = xf.shape
    tm = TM_PROJ
    tpb = seq // tm
    qg = jnp.tile(q_gain.reshape(1, HEAD_DIM), (1, HEADS_PER_GROUP))
    kg = jnp.tile(k_gain.reshape(3, HEAD_DIM), (1, KV_GROUPS))
    rows = lambda w: pl.BlockSpec((tm, w), lambda i: (i, 0))
    hm = lambda w: pl.BlockSpec((None, KV_GROUPS, tm, w), lambda i: (i // tpb, 0, i % tpb, 0))
    hshape = lambda w, dt: jax.ShapeDtypeStruct((bsz, KV_GROUPS, seq, w), dt)
    ng = 3 * HEADS_PER_GROUP
    return pl.pallas_call(
        functools.partial(_inproj_kernel, tiles_per_seq=tpb),
        grid=(n // tm,),
        in_specs=[
            rows(d),
            pl.BlockSpec((1, d), lambda i: (0, 0)),
            pl.BlockSpec((None, 6, d), lambda i: (i // tpb, 0, 0)),
            pl.BlockSpec((d, IN_PAD), lambda i: (0, 0), pipeline_mode=pl.Buffered(1)),
            pl.BlockSpec((1, HEADS_PER_GROUP * HEAD_DIM), lambda i: (0, 0)),
            pl.BlockSpec((3, KV_WIDTH), lambda i: (0, 0)),
        ],
        out_specs=[
            rows(NSA_WIDTH), hm(2 * HEAD_DIM), hm(HEAD_DIM), hm(HEAD_DIM), hm(HEAD_DIM), hm(ng),
            rows(2 * KV_WIDTH), rows(2 * LRU_WIDTH),
        ],
        out_shape=[
            jax.ShapeDtypeStruct((n, NSA_WIDTH), bf16),
            hshape(2 * HEAD_DIM, bf16), hshape(HEAD_DIM, bf16), hshape(HEAD_DIM, bf16), hshape(HEAD_DIM, bf16),
            hshape(ng, f32),
            jax.ShapeDtypeStruct((n, 2 * KV_WIDTH), f32), jax.ShapeDtypeStruct((n, 2 * LRU_WIDTH), f32),
        ],
        compiler_params=_cparams(56, 1),
        name="in_proj",
    )(xf, g1.reshape(1, d), mod3, w_pad, qg, kg)


def _compress_kernel(fk_ref, fv_ref, pek_ref, pev_ref, wk_ref, wv_ref, kg_ref, kc_ref, vc_ref):
    def blocks(f_ref, pe_ref, w_ref):
        x = f_ref[...]
        nxt = pltpu.roll(x, x.shape[0] - 1, axis=0)
        half = x.shape[1]
        lo = (x + pe_ref[:, 0:half]).astype(bf16)
        hi = (nxt + pe_ref[:, half:]).astype(bf16)
        return _dot(lo, w_ref[0:half, :]) + _dot(hi, w_ref[half:, :])

    kc_ref[...] = _rms_rows(blocks(fk_ref, pek_ref, wk_ref), kg_ref[...]).astype(bf16)
    vc_ref[...] = blocks(fv_ref, pev_ref, wv_ref).astype(bf16)


def _compress(zkv, pe_k, pe_v, w_ck, w_cv, k_gain0, bsz, seq):
    nsub = seq // CMP_STRIDE
    kdim = CMP_LEN * HEAD_DIM
    sub = zkv[:, 0:2 * KV_WIDTH].reshape(bsz, nsub, CMP_STRIDE, 2, KV_GROUPS, HEAD_DIM)
    sub = sub.transpose(3, 0, 4, 1, 2, 5).reshape(2, bsz, KV_GROUPS, nsub, kdim // 2)
    blk = lambda kv: pl.BlockSpec((None, None, None, nsub, kdim // 2), lambda b, g: (kv, b, g, 0, 0))
    cst = lambda r, c: pl.BlockSpec((r, c), lambda b, g: (0, 0))
    oblk = pl.BlockSpec((None, None, nsub, HEAD_DIM), lambda b, g: (b, g, 0, 0))
    oshape = jax.ShapeDtypeStruct((bsz, KV_GROUPS, nsub, HEAD_DIM), bf16)
    return pl.pallas_call(
        _compress_kernel,
        grid=(bsz, KV_GROUPS),
        in_specs=[blk(0), blk(1), cst(1, kdim), cst(1, kdim), cst(kdim, HEAD_DIM), cst(kdim, HEAD_DIM),
                  cst(1, HEAD_DIM)],
        out_specs=[oblk, oblk],
        out_shape=[oshape, oshape],
        compiler_params=_cparams(32, 2),
        name="nsa_compress",
    )(sub, sub, pe_k.reshape(1, kdim), pe_v.reshape(1, kdim), w_ck.astype(bf16), w_cv.astype(bf16),
      k_gain0.reshape(1, HEAD_DIM))


def _softmax_strips(s_ref, p_ref, r_ref, width, mask_from, key_ok):
    strips = [slice(i * STRIP, (i + 1) * STRIP) for i in range(s_ref.shape[0] // STRIP)]
    wide = lambda v: jnp.concatenate([v] * (width // LANES), axis=1)
    lanes = lambda v: jnp.broadcast_to(v, (STRIP, LANES))

    for r in strips:
        tail = jnp.where(key_ok(r.start % TQ, mask_from, width - mask_from), s_ref[r, mask_from:width], NEG_INF)
        s_ref[r, mask_from:width] = tail
        m = jnp.max(tail, axis=-1, keepdims=True)
        if mask_from:
            m = jnp.maximum(m, jnp.max(s_ref[r, 0:mask_from], axis=-1, keepdims=True))
        r_ref[r, :] = lanes(m)
    for r in strips:
        e = jnp.exp(s_ref[r, 0:width] - wide(r_ref[r, :]))
        s_ref[r, 0:width] = e
        r_ref[r, :] = lanes(1.0 / jnp.sum(e, axis=-1, keepdims=True))
    for r in strips:
        p_ref[r, 0:width] = (s_ref[r, 0:width] * wide(r_ref[r, :])).astype(bf16)


def _select_kernel(q_ref, kc_ref, vc_ref, oc_ref, sb_ref, *, seq):
    tq = q_ref.shape[0]
    t0 = pl.program_id(2) * tq
    hg = HEADS_PER_GROUP
    q = q_ref[...]
    qs = jnp.concatenate([q[:, h * HEAD_DIM:(h + 1) * HEAD_DIM] for h in range(hg)], axis=0)
    tpos4 = t0 + lax.rem(lax.broadcasted_iota(i32, (hg * tq, 1), 0), tq)

    ncmp = kc_ref.shape[0]
    nblk = seq // SEL_LEN
    s_c = _dot_nt(qs, kc_ref[...])
    cstart = lax.broadcasted_iota(i32, (1, ncmp), 1) * CMP_STRIDE
    p_c = _masked_softmax(s_c, (cstart + (CMP_LEN - 1)) <= tpos4).astype(bf16)
    o_c = _dot(p_c, vc_ref[...])
    oc_ref[...] = jnp.concatenate([o_c[h * tq:(h + 1) * tq] for h in range(hg)], axis=1)

    cs = lax.broadcasted_iota(i32, (nblk, ncmp), 1) * CMP_STRIDE
    ss = lax.broadcasted_iota(i32, (nblk, ncmp), 0) * SEL_LEN
    overlap_t = jnp.where((cs < ss + SEL_LEN) & (cs + CMP_LEN > ss), 1.0, 0.0).astype(bf16)
    imp4 = _dot_nt(overlap_t, p_c)
    imp = imp4[:, 0:tq]
    for h in range(1, hg):
        imp = imp + imp4[:, h * tq:(h + 1) * tq]
    j = lax.broadcasted_iota(i32, (nblk, tq), 0)
    qblk = jnp.right_shift(t0 + lax.broadcasted_iota(i32, (nblk, tq), 1), 6)
    forced = (j == 0) | (j == qblk) | (j == qblk - 1)
    impf = jnp.where(forced, SEL_FORCE, jnp.where(j <= qblk, imp, -SEL_FORCE))
    beaten = jnp.zeros((nblk, tq), i32)
    for i in range(nblk):
        ci = impf[i:i + 1, :]
        beats = (ci > impf) | ((ci == impf) & (j > i))
        beaten = beaten + jnp.where(beats, 1, 0)
    keep = (beaten < min(N_SEL, nblk)) & (j <= qblk)
    sel_bias = jnp.where(keep, 0.0, NEG_INF)
    if nblk < HEAD_DIM:
        sel_bias = jnp.concatenate([sel_bias, jnp.zeros((HEAD_DIM - nblk, tq), f32)], axis=0)
    sb_ref[...] = sel_bias.T.astype(bf16)


def _select(qn, kc, vc, bsz, seq):
    n = qn.shape[0]
    gw = HEADS_PER_GROUP * HEAD_DIM
    nt = seq // TQ_SELECT
    nsub = seq // CMP_STRIDE
    qblk = pl.BlockSpec((TQ_SELECT, gw), lambda b, g, t: (b * nt + t, g))
    full = pl.BlockSpec((None, None, nsub, HEAD_DIM), lambda b, g, t: (b, g, 0, 0))
    return pl.pallas_call(
        functools.partial(_select_kernel, seq=seq),
        grid=(bsz, KV_GROUPS, nt),
        in_specs=[qblk, full, full],
        out_specs=[qblk, pl.BlockSpec((None, None, TQ_SELECT, HEAD_DIM), lambda b, g, t: (b, g, t, 0))],
        out_shape=[
            jax.ShapeDtypeStruct((n, NSA_WIDTH), f32),
            jax.ShapeDtypeStruct((bsz, KV_GROUPS, seq, HEAD_DIM), bf16),
        ],
        compiler_params=_cparams(40, 3),
        name="nsa_select",
    )(qn, kc, vc)


def _attn_kernel(q_ref, oc_ref, sb_ref, ks_ref, vs_ref, kw_ref, vw_ref, gate_ref, o_ref,
                 os_ref, s_ref, p_ref, r_ref, *, seq):
    t = pl.program_id(2)
    hg = HEADS_PER_GROUP
    q = q_ref[...]
    qs = jnp.concatenate([q[:, h * HEAD_DIM:(h + 1) * HEAD_DIM] for h in range(hg)], axis=0)
    t0 = t * TQ
    halves = [slice(0, hg // 2 * TQ), slice(hg // 2 * TQ, hg * TQ)]
    sel_bias = sb_ref[...]
    q_aug = jnp.concatenate(
        [jnp.concatenate([q[:, h * HEAD_DIM:(h + 1) * HEAD_DIM], sel_bias], axis=1) for h in range(hg)], axis=0)

    def causal_ok(tok0, col0, ncols):
        tpos_s = t0 + tok0 + lax.broadcasted_iota(i32, (STRIP, 1), 0)
        return (col0 + lax.broadcasted_iota(i32, (1, ncols), 1)) <= tpos_s

    n_chunks = (t0 + TQ + KEY_CHUNK - 1) // KEY_CHUNK
    for c in range(seq // KEY_CHUNK):

        @pl.when(n_chunks == c + 1)
        def _(width=(c + 1) * KEY_CHUNK):
            for rows in halves:
                s_ref[rows, 0:width] = _dot_nt(q_aug[rows], ks_ref[0:width, :])
            _softmax_strips(s_ref, p_ref, r_ref, width, width - KEY_CHUNK, causal_ok)
            for rows in halves:
                os_ref[rows, :] = _dot(p_ref[rows, 0:width], vs_ref[0:width, :])

    o_s = os_ref[...]

    wk = WINDOW + TQ
    start = pl.multiple_of(jnp.maximum(t0 - WINDOW, 0), TQ)

    def window_ok(tok0, col0, ncols):
        tpos_s = t0 + tok0 + lax.broadcasted_iota(i32, (STRIP, 1), 0)
        wpos = start + col0 + lax.broadcasted_iota(i32, (1, ncols), 1)
        return (wpos <= tpos_s) & (wpos > tpos_s - WINDOW)

    for rows in halves:
        s_ref[rows, 0:wk] = _dot_nt(qs[rows], kw_ref[pl.ds(start, wk), :])
    _softmax_strips(s_ref, p_ref, r_ref, wk, 0, window_ok)
    o_w = jnp.concatenate([_dot(p_ref[rows, 0:wk], vw_ref[pl.ds(start, wk), :]) for rows in halves], axis=0)

    gts = gate_ref[...]
    o_c = oc_ref[...]
    outs = []
    for h in range(hg):
        rows = slice(h * TQ, (h + 1) * TQ)
        outs.append(gts[:, 3 * h:3 * h + 1] * o_c[:, h * HEAD_DIM:(h + 1) * HEAD_DIM]
                    + gts[:, 3 * h + 1:3 * h + 2] * o_s[rows] + gts[:, 3 * h + 2:3 * h + 3] * o_w[rows])
    o_ref[...] = jnp.concatenate(outs, axis=1)


def _attention(qn, o_cmp, sel_bias, ks, vs, kw, vw, gates, bsz, seq):
    assert KEY_CHUNK % TQ == 0 and seq % KEY_CHUNK == 0
    n = qn.shape[0]
    gw = HEADS_PER_GROUP * HEAD_DIM
    ntq = seq // TQ
    ng = 3 * HEADS_PER_GROUP
    qblk = pl.BlockSpec((TQ, gw), lambda b, g, t: (b * ntq + t, g))
    full = lambda r, w: pl.BlockSpec((None, None, r, w), lambda b, g, t: (b, g, 0, 0))
    return pl.pallas_call(
        functools.partial(_attn_kernel, seq=seq),
        grid=(bsz, KV_GROUPS, ntq),
        in_specs=[
            qblk, qblk, pl.BlockSpec((None, None, TQ, HEAD_DIM), lambda b, g, t: (b, g, t, 0)),
            full(seq, 2 * HEAD_DIM), full(seq, HEAD_DIM), full(seq, HEAD_DIM), full(seq, HEAD_DIM),
            pl.BlockSpec((None, None, TQ, ng), lambda b, g, t: (b, g, t, 0)),
        ],
        out_specs=qblk,
        out_shape=jax.ShapeDtypeStruct((n, NSA_WIDTH), f32),
        scratch_shapes=[
            pltpu.VMEM((HEADS_PER_GROUP * TQ, HEAD_DIM), f32),
            pltpu.VMEM((HEADS_PER_GROUP * TQ, seq), f32),
            pltpu.VMEM((HEADS_PER_GROUP * TQ, seq), bf16),
            pltpu.VMEM((HEADS_PER_GROUP * TQ, LANES), f32),
        ],
        compiler_params=_cparams(48, 3),
        name="nsa_attention",
    )(qn, o_cmp, sel_bias, ks, vs, kw, vw, gates)


def _lru_kernel(xr_ref, xg_ref, cw_ref, cb_ref, wr_ref, br_ref, wi_ref, bi_ref, lam_ref, o_ref,
                xbuf, hstate, abuf, ubuf):
    ts = xr_ref.shape[0]
    s = pl.program_id(1)

    @pl.when(s == 0)
    def _():
        xbuf[0:8, :] = jnp.zeros((8, LRU_WIDTH), f32)
        hstate[...] = jnp.zeros_like(hstate)

    xbuf[8:8 + ts, :] = xr_ref[...]
    acc = xbuf[5:5 + ts, :] * cw_ref[0:1, :]
    for k in range(1, CONV_W):
        acc = acc + xbuf[5 + k:5 + k + ts, :] * cw_ref[k:k + 1, :]
    xc = cb_ref[...] + acc
    xbuf[0:8, :] = xbuf[ts:ts + 8, :]

    xcb = xc.astype(bf16)
    r = jax.nn.sigmoid(_dot(xcb, wr_ref[...]) + br_ref[...])
    ig = jax.nn.sigmoid(_dot(xcb, wi_ref[...]) + bi_ref[...])
    nl = -lam_ref[...]
    softplus = jnp.maximum(nl, 0.0) + jnp.log1p(jnp.exp(-jnp.abs(nl)))
    log_a = (-RG_C * softplus) * r
    abuf[...] = jnp.exp(log_a)
    th = jnp.tanh(log_a)
    ubuf[...] = jnp.sqrt(-2.0 * th / (1.0 - th)) * (ig * xc)

    def step(jb, h):
        base = pl.multiple_of(jb * 8, 8)
        a8 = abuf[pl.ds(base, 8), :]
        u8 = ubuf[pl.ds(base, 8), :]
        rows = []
        for rr in range(8):
            h = a8[rr:rr + 1, :] * h + u8[rr:rr + 1, :]
            rows.append(h)
        ubuf[pl.ds(base, 8), :] = jnp.concatenate(rows, axis=0)
        return h

    hstate[0:1, :] = lax.fori_loop(0, ts // 8, step, hstate[0:1, :])
    xg = xg_ref[...]
    gelu = 0.5 * xg * (1.0 + jnp.tanh(0.7978845608028654 * (xg + 0.044715 * (xg * xg * xg))))
    o_ref[...] = ubuf[...] * gelu


def _block_diag(w):
    nb, bw, _ = w.shape
    n = nb * bw
    tiled = jnp.tile(w.reshape(n, bw), (1, nb))
    same = (lax.broadcasted_iota(i32, (n, n), 0) // bw) == (lax.broadcasted_iota(i32, (n, n), 1) // bw)
    return jnp.where(same, tiled, 0.0)


def _lru(zx, conv_w, conv_b, w_rg, b_rg, w_ig, b_ig, lam, bsz, seq):
    n = zx.shape[0]
    ts = TS_LRU
    tpb = seq // ts
    w = LRU_WIDTH
    row = lambda v: v.reshape(1, w)
    cst = lambda r, c: pl.BlockSpec((r, c), lambda b, s: (0, 0))
    return pl.pallas_call(
        _lru_kernel,
        grid=(bsz, tpb),
        in_specs=[
            pl.BlockSpec((ts, w), lambda b, s: (b * tpb + s, 0)),
            pl.BlockSpec((ts, w), lambda b, s: (b * tpb + s, 1)),
            cst(CONV_W, w), cst(1, w), cst(w, w), cst(1, w), cst(w, w), cst(1, w), cst(1, w),
        ],
        out_specs=pl.BlockSpec((ts, w), lambda b, s: (b * tpb + s, 0)),
        out_shape=jax.ShapeDtypeStruct((n, w), f32),
        scratch_shapes=[
            pltpu.VMEM((ts + 8, w), f32), pltpu.VMEM((8, w), f32),
            pltpu.VMEM((ts, w), f32), pltpu.VMEM((ts, w), f32),
        ],
        compiler_params=_cparams(40, 2),
        name="rglru",
    )(zx, zx, conv_w, row(conv_b), _block_diag(w_rg).astype(bf16), row(b_rg),
      _block_diag(w_ig).astype(bf16), row(b_ig), row(lam))


def _outproj_kernel(on_ref, ol_ref, x_ref, mod_ref, gn_ref, gl_ref, w_ref, g2_ref, wrh_ref, wrl_ref, br_ref,
                    x1_ref, hp_ref, te_ref, tg_ref):
    n_rows = x_ref.shape[0]
    for rows in (slice(0, n_rows // 2), slice(n_rows // 2, n_rows)):
        _outproj_rows(rows, on_ref, ol_ref, x_ref, mod_ref, gn_ref, gl_ref, w_ref, g2_ref, wrh_ref, wrl_ref, br_ref,
                      x1_ref, hp_ref, te_ref, tg_ref)


def _outproj_rows(rows, on_ref, ol_ref, x_ref, mod_ref, gn_ref, gl_ref, w_ref, g2_ref, wrh_ref, wrl_ref, br_ref,
                  x1_ref, hp_ref, te_ref, tg_ref):
    nn = _rms_rows(on_ref[rows, :], gn_ref[...])
    nl = _rms_rows(ol_ref[rows, :], gl_ref[...])
    hcat = jnp.concatenate([nn, nl], axis=1).astype(bf16)
    mix = _dot(hcat, w_ref[...])
    x1 = x_ref[rows, :] + mod_ref[2:3, :] * mix
    x1_ref[rows, :] = x1
    h2 = _rms_rows(x1, g2_ref[...]) * (1.0 + mod_ref[4:5, :]) + mod_ref[3:4, :]

    half = D_MODEL // 2
    hb = h2.astype(bf16).astype(f32)
    hi_bits = lax.bitcast_convert_type(hb[:, :half], i32)
    lo_bits = lax.shift_right_logical(lax.bitcast_convert_type(hb[:, half:], i32), 16)
    hp_ref[rows, :] = hi_bits | lo_bits

    hh, hl = _split_bf16(h2)
    logits = _dot(hh, wrh_ref[...]) + _dot(hl, wrh_ref[...]) + _dot(hh, wrl_ref[...]) + br_ref[...]
    lane = lax.broadcasted_iota(i32, (1, LANES), 1)
    lane_f = lane.astype(f32)
    vals, idxs = [], []
    cur = logits
    for _ in range(TOP_K):
        m = jnp.max(cur, axis=-1, keepdims=True)
        idx = jnp.min(jnp.where(cur == m, lane_f, float(LANES)), axis=-1, keepdims=True).astype(i32)
        vals.append(m)
        idxs.append(idx)
        cur = jnp.where(lane == idx, -3e38, cur)
    es = [jnp.exp(v - vals[0]) for v in vals]
    den = es[0]
    for e in es[1:]:
        den = den + e
    inv = 1.0 / den
    te = jnp.full(logits.shape, -1, i32)
    tg = jnp.zeros(logits.shape, f32)
    for k in range(TOP_K):
        te = jnp.where(lane == k, idxs[k], te)
        tg = jnp.where(lane == k, es[k] * inv, tg)
    te_ref[rows, :] = te
    tg_ref[rows, :] = tg


def _outproj(o_nsa, o_lru, xf, mod3, g_out_nsa, g_out_lru, w_out, g_norm2, w_router, b_router, seq):
    n, d = xf.shape
    tm = TM_OUT
    tpb = seq // tm
    wr = jnp.zeros((d, LANES), f32).at[:, :N_EXPERTS].set(w_router)
    wrh = wr.astype(bf16)
    wrl = (wr - wrh.astype(f32)).astype(bf16)
    br = jnp.full((1, LANES), NEG_INF, f32).at[0, :N_EXPERTS].set(b_router)
    rows = lambda w: pl.BlockSpec((tm, w), lambda i: (i, 0))
    cst = lambda r, c: pl.BlockSpec((r, c), lambda i: (0, 0))
    return pl.pallas_call(
        _outproj_kernel,
        grid=(n // tm,),
        in_specs=[
            rows(NSA_WIDTH), rows(LRU_WIDTH), rows(d),
            pl.BlockSpec((None, 6, d), lambda i: (i // tpb, 0, 0)),
            cst(1, NSA_WIDTH), cst(1, LRU_WIDTH), cst(d, d), cst(1, d), cst(d, LANES), cst(d, LANES), cst(1, LANES),
        ],
        out_specs=[rows(d), rows(d // 2), rows(LANES), rows(LANES)],
        out_shape=[
            jax.ShapeDtypeStruct((n, d), f32), jax.ShapeDtypeStruct((n, d // 2), i32),
            jax.ShapeDtypeStruct((n, LANES), i32), jax.ShapeDtypeStruct((n, LANES), f32),
        ],
        compiler_params=_cparams(56, 1),
        name="out_proj_router",
    )(o_nsa, o_lru, xf, mod3, g_out_nsa.reshape(1, -1), g_out_lru.reshape(1, -1), w_out.astype(bf16),
      g_norm2.reshape(1, d), wrh, wrl, br)


def _rank_kernel(te_ref, rank_ref, cnt_ref, carry):
    i = pl.program_id(0)
    tm = te_ref.shape[0]

    @pl.when(i == 0)
    def _():
        carry[...] = jnp.zeros_like(carry)

    te = te_ref[...]
    lane = lax.broadcasted_iota(i32, (1, LANES), 1)
    hits = [te[:, k:k + 1] == lane for k in range(TOP_K)]
    onehot = jnp.zeros((tm, LANES), f32)
    for h in hits:
        onehot = onehot + jnp.where(h, 1.0, 0.0)
    r = lax.broadcasted_iota(i32, (tm, tm), 0)
    c = lax.broadcasted_iota(i32, (tm, tm), 1)
    lower = jnp.where(c < r, 1.0, 0.0).astype(bf16)
    prefix = _dot(lower, onehot.astype(bf16)) + carry[0:1, :]
    rank = jnp.zeros((tm, LANES), f32)
    for k in range(TOP_K):
        rk = jnp.sum(jnp.where(hits[k], prefix, 0.0), axis=-1, keepdims=True)
        rank = jnp.where(lane == k, rk, rank)
    rank_ref[...] = rank.astype(i32)
    carry[0:1, :] = carry[0:1, :] + jnp.sum(onehot, axis=0, keepdims=True)
    cnt_ref[...] = carry[...]


def _ranks(te_pad):
    n = te_pad.shape[0]
    tm = TM_ROUTE
    return pl.pallas_call(
        _rank_kernel,
        grid=(n // tm,),
        in_specs=[pl.BlockSpec((tm, LANES), lambda i: (i, 0))],
        out_specs=[pl.BlockSpec((tm, LANES), lambda i: (i, 0)), pl.BlockSpec((8, LANES), lambda i: (0, 0))],
        out_shape=[jax.ShapeDtypeStruct((n, LANES), i32), jax.ShapeDtypeStruct((8, LANES), f32)],
        scratch_shapes=[pltpu.VMEM((8, LANES), f32)],
        compiler_params=_cparams(32, 1),
        name="route_ranks",
    )(te_pad)


def _slot_kernel(te_ref, rank_ref, pstart_ref, dest_ref):
    te = te_ref[...]
    lane = lax.broadcasted_iota(i32, (1, LANES), 1)
    ps = pstart_ref[...].astype(f32)
    dest = jnp.zeros(te.shape, i32)
    for k in range(TOP_K):
        base = jnp.sum(jnp.where(te[:, k:k + 1] == lane, ps, 0.0), axis=-1, keepdims=True)
        dest = jnp.where(lane == k, base.astype(i32), dest)
    dest_ref[...] = dest + rank_ref[...]


def _slots(te_pad, rank_pad, pstart_row):
    n = te_pad.shape[0]
    tm = TM_ROUTE
    blk = pl.BlockSpec((tm, LANES), lambda i: (i, 0))
    return pl.pallas_call(
        _slot_kernel,
        grid=(n // tm,),
        in_specs=[blk, blk, pl.BlockSpec((1, LANES), lambda i: (0, 0))],
        out_specs=blk,
        out_shape=jax.ShapeDtypeStruct((n, LANES), i32),
        compiler_params=_cparams(32, 1),
        name="route_slots",
    )(te_pad, rank_pad, pstart_row)


def _row_copy(src, src_row, dst, dst_row, sem):
    return pltpu.make_async_copy(src.at[pl.ds(src_row, 1)], dst.at[pl.ds(dst_row, 1)], sem)


def _dispatch_kernel(dest_ref, fill_ref, h_ref, xs_ref, zbuf, sem, zsem):
    tm = h_ref.shape[0]

    @pl.when(pl.program_id(0) == 0)
    def _():
        zbuf[...] = jnp.zeros_like(zbuf)

        def fill(e, c):
            start = pl.multiple_of(fill_ref[e], 8)
            cp = pltpu.make_async_copy(zbuf, xs_ref.at[pl.ds(start, FILL_ROWS)], zsem)
            cp.start()
            cp.wait()
            return c

        lax.fori_loop(0, fill_ref.shape[0], fill, 0)

    def issue(i, c):
        for k in range(TOP_K):
            _row_copy(h_ref, i, xs_ref, dest_ref[0, i * TOP_K + k], sem).start(priority=k % 2)
        return c

    lax.fori_loop(0, tm, issue, 0)

    def drain(i, c):
        for k in range(TOP_K):
            _row_copy(h_ref, 0, xs_ref, 0, sem).wait()
        return c

    lax.fori_loop(0, tm, drain, 0)


def _dispatch(dest_tiles, fill_start, hp, n_slots):
    n, w = hp.shape
    tm = TM_DISP
    return pl.pallas_call(
        _dispatch_kernel,
        grid=(n // tm,),
        in_specs=[
            pl.BlockSpec((None, 1, tm * TOP_K), lambda i: (i, 0, 0), memory_space=pltpu.SMEM),
            pl.BlockSpec(memory_space=pltpu.SMEM),
            pl.BlockSpec((tm, w), lambda i: (i, 0)),
        ],
        out_specs=pl.BlockSpec(memory_space=pl.ANY),
        out_shape=jax.ShapeDtypeStruct((n_slots, w), i32),
        scratch_shapes=[pltpu.VMEM((FILL_ROWS, w), i32), pltpu.SemaphoreType.DMA(()), pltpu.SemaphoreType.DMA(())],
        compiler_params=_cparams(32, 1),
        name="moe_dispatch",
    )(dest_tiles, fill_start, hp)


def _unpack_rows(xp):
    hi = lax.bitcast_convert_type(xp & jnp.int32(-65536), f32)
    lo = lax.bitcast_convert_type(lax.shift_left(xp, 16), f32)
    return jnp.concatenate([hi, lo], axis=1).astype(bf16)


def _stage_weights(i, n_live, be_ref, nxt_ref, slot_ref, copies, cast):
    first = (i < n_live) & ((i == 0) | (be_ref[i] != be_ref[jnp.maximum(i - 1, 0)]))
    slot = slot_ref[i]

    @pl.when(first & (i == 0))
    def _():
        for cp in copies(be_ref[i], slot):
            cp.start()

    @pl.when(first)
    def _():
        for cp in copies(be_ref[i], slot):
            cp.wait()

        @pl.when(nxt_ref[i] >= 0)
        def _():
            for cp in copies(nxt_ref[i], 1 - slot):
                cp.start()

        cast(slot)


def _up_kernel(be_ref, nb_ref, nxt_ref, slot_ref, xs_ref, w_hbm, bg_ref, bl_ref, act_ref, stage, wgb, wlb, sem):
    n = pl.program_id(0)
    i = pl.program_id(1)
    live = i < nb_ref[0]
    tf = wgb.shape[1]

    def copies(e, s):
        cols = [pl.ds(pl.multiple_of(half * D_FF + n * tf, LANES), tf) for half in range(2)]
        return [pltpu.make_async_copy(w_hbm.at[e, :, cols[half]], stage.at[s, half], sem.at[s]) for half in range(2)]

    def cast(s):
        wgb[...] = stage[s, 0].astype(bf16)
        wlb[...] = stage[s, 1].astype(bf16)

    _stage_weights(i, nb_ref[0], be_ref, nxt_ref, slot_ref, copies, cast)

    @pl.when(live)
    def _():
        x = _unpack_rows(xs_ref[...])
        ug = _dot(x, wgb[...]) + bg_ref[...]
        ul = _dot(x, wlb[...]) + bl_ref[...]
        ug = jnp.minimum(ug, SWIGLU_LIMIT)
        ul = jnp.clip(ul, -SWIGLU_LIMIT, SWIGLU_LIMIT)
        act_ref[...] = (ug * jax.nn.sigmoid(SWIGLU_ALPHA * ug) * (ul + 1.0)).astype(bf16)

    @pl.when(pl.program_id(1) >= nb_ref[0])
    def _():
        act_ref[...] = jnp.zeros_like(act_ref)


def _expert_up(route, xs, w_e1, b_e1):
    n_slots, w = xs.shape
    nblk = n_slots // MOE_BLK
    tf = TF_UP
    ncol = D_FF // tf
    d = D_MODEL
    grid_spec = pltpu.PrefetchScalarGridSpec(
        num_scalar_prefetch=4,
        grid=(ncol, nblk),
        in_specs=[
            pl.BlockSpec((MOE_BLK, w), lambda n, i, be, nb, nx, sl: (jnp.minimum(i, nb[0] - 1), 0)),
            pl.BlockSpec(memory_space=pl.ANY),
            pl.BlockSpec((None, 1, tf), lambda n, i, be, nb, nx, sl: (be[i], 0, n)),
            pl.BlockSpec((None, 1, tf), lambda n, i, be, nb, nx, sl: (be[i], 0, ncol + n)),
        ],
        out_specs=pl.BlockSpec((MOE_BLK, tf), lambda n, i, be, nb, nx, sl: (i, n)),
        scratch_shapes=[
            pltpu.VMEM((2, 2, d, tf), f32), pltpu.VMEM((d, tf), bf16), pltpu.VMEM((d, tf), bf16),
            pltpu.SemaphoreType.DMA((2,)),
        ],
    )
    return pl.pallas_call(
        _up_kernel,
        grid_spec=grid_spec,
        out_shape=jax.ShapeDtypeStruct((n_slots, D_FF), bf16),
        compiler_params=_cparams(56, 2),
        name="moe_up",
    )(*route, xs, w_e1, b_e1.reshape(N_EXPERTS, 1, 2 * D_FF), b_e1.reshape(N_EXPERTS, 1, 2 * D_FF))


def _down_kernel(be_ref, nb_ref, nxt_ref, slot_ref, act_ref, w_hbm, b_ref, y_ref, stage, wb, sem):
    i = pl.program_id(1)
    live = i < nb_ref[0]

    def copies(e, s):
        return [pltpu.make_async_copy(w_hbm.at[e], stage.at[s], sem.at[s])]

    def cast(s):
        wb[...] = stage[s].astype(bf16)

    _stage_weights(i, nb_ref[0], be_ref, nxt_ref, slot_ref, copies, cast)

    @pl.when(live)
    def _():
        y_ref[...] = _dot(act_ref[...], wb[...]) + b_ref[...]

    @pl.when(pl.program_id(1) >= nb_ref[0])
    def _():
        y_ref[...] = jnp.zeros_like(y_ref)


def _expert_down(route, act, w_e2, b_e2):
    n_slots = act.shape[0]
    nblk = n_slots // MOE_BLK
    grid_spec = pltpu.PrefetchScalarGridSpec(
        num_scalar_prefetch=4,
        grid=(1, nblk),
        in_specs=[
            pl.BlockSpec((MOE_BLK, D_FF), lambda n, i, be, nb, nx, sl: (i, 0)),
            pl.BlockSpec(memory_space=pl.ANY),
            pl.BlockSpec((None, 1, D_MODEL), lambda n, i, be, nb, nx, sl: (be[i], 0, 0)),
        ],
        out_specs=pl.BlockSpec((MOE_BLK, D_MODEL), lambda n, i, be, nb, nx, sl: (i, 0)),
        scratch_shapes=[
            pltpu.VMEM((2, D_FF, D_MODEL), f32), pltpu.VMEM((D_FF, D_MODEL), bf16), pltpu.SemaphoreType.DMA((2,)),
        ],
    )
    return pl.pallas_call(
        _down_kernel,
        grid_spec=grid_spec,
        out_shape=jax.ShapeDtypeStruct((n_slots, D_MODEL), f32),
        compiler_params=_cparams(56, 2),
        name="moe_down",
    )(*route, act, w_e2, b_e2.reshape(N_EXPERTS, 1, D_MODEL))


def _combine_kernel(dest_ref, dest_next_ref, y_ref, x1_ref, tg_ref, mod_ref, o_ref, buf, sem):
    tm = x1_ref.shape[0]
    step = pl.program_id(0)
    slot = step % 2

    def gather(d_ref, s):
        def issue(i, c):
            for k in range(TOP_K):
                _row_copy(y_ref, d_ref[0, i * TOP_K + k], buf.at[s, k], i, sem.at[s]).start(priority=k % 2)
            return c

        lax.fori_loop(0, tm, issue, 0)

    @pl.when(step == 0)
    def _():
        gather(dest_ref, 0)

    @pl.when(step + 1 < pl.num_programs(0))
    def _():
        gather(dest_next_ref, 1 - slot)

    def drain(i, c):
        for k in range(TOP_K):
            _row_copy(y_ref, 0, buf.at[slot, k], 0, sem.at[slot]).wait()
        return c

    lax.fori_loop(0, tm, drain, 0)
    tg = tg_ref[...]
    acc = tg[:, 0:1] * buf[slot, 0]
    for k in range(1, TOP_K):
        acc = acc + tg[:, k:k + 1] * buf[slot, k]
    o_ref[...] = x1_ref[...] + mod_ref[5:6, :] * acc


def _combine(dest_tiles, y, x1, tg_pad, mod3, seq):
    n, d = x1.shape
    tm = TM_COMB
    tpb = seq // tm
    return pl.pallas_call(
        _combine_kernel,
        grid=(n // tm,),
        in_specs=[
            pl.BlockSpec((None, 1, tm * TOP_K), lambda i: (i, 0, 0), memory_space=pltpu.SMEM),
            pl.BlockSpec((None, 1, tm * TOP_K), lambda i: (jnp.minimum(i + 1, n // tm - 1), 0, 0),
                         memory_space=pltpu.SMEM),
            pl.BlockSpec(memory_space=pl.ANY),
            pl.BlockSpec((tm, d), lambda i: (i, 0)),
            pl.BlockSpec((tm, LANES), lambda i: (i, 0)),
            pl.BlockSpec((None, 6, d), lambda i: (i // tpb, 0, 0)),
        ],
        out_specs=pl.BlockSpec((tm, d), lambda i: (i, 0)),
        out_shape=jax.ShapeDtypeStruct((n, d), f32),
        scratch_shapes=[pltpu.VMEM((2, TOP_K, tm, d), f32), pltpu.SemaphoreType.DMA((2,))],
        compiler_params=_cparams(40, 1),
        name="moe_combine",
    )(dest_tiles, dest_tiles, y, x1, tg_pad, mod3)


def _layer(x, mod, g_norm1, w_in, pe_cmp_k, pe_cmp_v, w_cmp_k, w_cmp_v, q_gain, k_gain, conv_w, conv_b,
           w_rg, b_rg, w_ig, b_ig, lru_lambda, g_out_nsa, g_out_lru, w_out, g_norm2, w_router, b_router,
           w_e1, b_e1, w_e2, b_e2):
    bsz, seq, d = x.shape
    n = bsz * seq
    xf = x.reshape(n, d)
    mod3 = mod.reshape(bsz, 6, d)

    gate_col = NSA_WIDTH + 6 * KV_WIDTH
    n_gate = 3 * N_HEADS
    w_pad = jnp.concatenate(
        [w_in[:, :gate_col + n_gate], jnp.zeros((d, GATE_PAD - n_gate), w_in.dtype), w_in[:, gate_col + n_gate:]],
        axis=1).astype(bf16)
    qn, ks, vs, kw, vw, gates, zc, zx = _inproj(xf, g_norm1, mod3, w_pad, q_gain, k_gain, bsz, seq)
    kc, vc = _compress(zc, pe_cmp_k, pe_cmp_v, w_cmp_k, w_cmp_v, k_gain[0], bsz, seq)
    o_cmp, sel_bias = _select(qn, kc, vc, bsz, seq)
    o_nsa = _attention(qn, o_cmp, sel_bias, ks, vs, kw, vw, gates, bsz, seq)
    o_lru = _lru(zx, conv_w, conv_b, w_rg, b_rg, w_ig, b_ig, lru_lambda, bsz, seq)

    x1, hp, te_pad, tg_pad = _outproj(o_nsa, o_lru, xf, mod3, g_out_nsa, g_out_lru, w_out, g_norm2,
                                      w_router, b_router, seq)

    rank_pad, cnt = _ranks(te_pad)
    counts = cnt[0, :N_EXPERTS].astype(i32)
    pcounts = (counts + MOE_BLK - 1) // MOE_BLK * MOE_BLK
    pends = jnp.cumsum(pcounts)
    pstarts = pends - pcounts
    n_blocks = (n * TOP_K + N_EXPERTS * (MOE_BLK - 1) + MOE_BLK - 1) // MOE_BLK
    n_slots = n_blocks * MOE_BLK
    blk_start = jnp.arange(n_blocks, dtype=i32) * MOE_BLK
    blk_e = jnp.minimum(jnp.sum((pends[None, :] <= blk_start[:, None]).astype(i32), axis=1), N_EXPERTS - 1)
    n_used = (pends[-1] // MOE_BLK).astype(i32).reshape(1)
    n_tail = (n_slots - n * TOP_K + MOE_BLK - 1) // MOE_BLK
    fill_rows = jnp.concatenate([pstarts + counts, pends[-1] + jnp.arange(n_tail, dtype=i32) * MOE_BLK])
    fill_start = jnp.minimum(fill_rows // 8 * 8, n_slots - FILL_ROWS).astype(i32)
    pstart_row = jnp.zeros((1, LANES), i32).at[0, :N_EXPERTS].set(pstarts.astype(i32))
    dest_pad = _slots(te_pad, rank_pad, pstart_row)
    dest = dest_pad[:, :TOP_K]

    xs = _dispatch(dest.reshape(n // TM_DISP, 1, TM_DISP * TOP_K), fill_start, hp, n_slots)
    run_first = jnp.concatenate([jnp.ones((1,), bool), blk_e[1:] != blk_e[:-1]])
    slot = ((jnp.cumsum(run_first.astype(i32)) - 1) % 2).astype(i32)
    later = lax.cummin(jnp.where(counts > 0, jnp.arange(N_EXPERTS, dtype=i32), N_EXPERTS), reverse=True)
    nxt_e = jnp.concatenate([later[1:], jnp.full((1,), N_EXPERTS, i32)])
    nxt = jnp.where(nxt_e < N_EXPERTS, nxt_e, -1)[blk_e].astype(i32)
    route = (blk_e, n_used, nxt, slot)
    act = _expert_up(route, xs, w_e1, b_e1)
    y = _expert_down(route, act, w_e2, b_e2)
    out = _combine(dest.reshape(n // TM_COMB, 1, TM_COMB * TOP_K), y, x1, tg_pad, mod3, seq)
    return out.reshape(bsz, seq, d)


def kernel(x, c, w_ada, b_ada, g_norm1, w_in, pe_cmp_k, pe_cmp_v, w_cmp_k, w_cmp_v, q_gain, k_gain, conv_w, conv_b, w_rg, b_rg, w_ig, b_ig, lru_lambda, g_out_nsa, g_out_lru, w_out, g_norm2, w_router, b_router, w_e1, b_e1, w_e2, b_e2):
    for l in range(w_ada.shape[0]):
        mod = _ada_mod(c, w_ada[l], b_ada[l])
        x = _layer(x, mod, g_norm1[l], w_in[l], pe_cmp_k[l], pe_cmp_v[l], w_cmp_k[l], w_cmp_v[l], q_gain[l],
                   k_gain[l], conv_w[l], conv_b[l], w_rg[l], b_rg[l], w_ig[l], b_ig[l], lru_lambda[l],
                   g_out_nsa[l], g_out_lru[l], w_out[l], g_norm2[l], w_router[l], b_router[l], w_e1[l], b_e1[l],
                   w_e2[l], b_e2[l])
    return x
```
